```python
import math
import jax, jax.numpy as jnp
from jax import lax
import numpy as np

D_MODEL = 1024
BATCH = 4
SEQ = 4096
DEPTH = 2
DEC_BATCH = 32
DEC_SEQ = 4
PAST_LEN = 16384
PAGE_SIZE = 128

N_CHUNK_LAYERS = (DEPTH + 1) // 2
N_ATTN_LAYERS = DEPTH // 2
CHUNK = 128
A_WIDTH = D_MODEL
A_GROUPS = 8
A_GROUP_DIM = A_WIDTH // A_GROUPS
N_HEADS = 8
HEAD_DIM = D_MODEL // N_HEADS
MOBA_BLOCK = 256
MOBA_TOPK = 3
Q_BLOCK = 32
ATTN_SCALE = HEAD_DIM ** -0.5
D_FF = 2816
N_EXPERTS = 8
TOP_K_EXPERTS = 2
D_FF_EXPERT = 2816
DEEPNORM_ALPHA = (2 * DEPTH) ** 0.25
DEEPNORM_BETA = (8 * DEPTH) ** -0.25
LN_EPS = 1e-5
POOL_FACTOR = 1.25

kernel_name = 'hybrid_gmlp_moba_deepnorm_step'


def layer_norm(x, g, b):
    xf = x.astype(jnp.float32)
    mu = jnp.mean(xf, axis=-1, keepdims=True)
    var = jnp.mean(jnp.square(xf - mu), axis=-1, keepdims=True)
    y = (xf - mu) * lax.rsqrt(var + LN_EPS) * g.astype(jnp.float32) + b.astype(jnp.float32)
    return y.astype(x.dtype)


def chunk_mlp(x, w_in, ln_g, ln_b, w_s, b_s, w_out):
    bsz, seq, _ = x.shape
    h = jax.nn.gelu(x @ w_in, approximate=False)
    u, v = jnp.split(h, 2, axis=-1)
    v = layer_norm(v, ln_g, ln_b)
    L = min(seq, CHUNK)
    causal = jnp.tril(jnp.ones((L, L), dtype=bool))
    w = jnp.where(causal, w_s[:, :L, :L], 0).astype(v.dtype)
    vc = v.reshape(bsz, seq // L, L, A_GROUPS, A_GROUP_DIM)
    z = jnp.einsum('gij,bcjgd->bcigd', w, vc) + jnp.transpose(b_s[:, :L]).astype(v.dtype)[None, None, :, :, None]
    y = (u * z.reshape(bsz, seq, A_WIDTH)) @ w_out
    return y, v


def _moba_combine(s_own, v_own, s_past, v_past):
    if s_past is None:
        p = jax.nn.softmax(s_own, axis=-1).astype(v_own.dtype)
        return jnp.einsum('bhqk,bhkd->bhqd', p, v_own)
    n_past = s_past.shape[-1]
    p = jax.nn.softmax(jnp.concatenate([s_past, s_own], axis=-1), axis=-1).astype(v_own.dtype)
    return (jnp.einsum('bhqk,bhqkd->bhqd', p[..., :n_past], v_past)
            + jnp.einsum('bhqk,bhkd->bhqd', p[..., n_past:], v_own))


def _qkv(x, w_qkv):
    bsz, seq, _ = x.shape
    qkv = (x @ w_qkv).reshape(bsz, seq, 3, N_HEADS, HEAD_DIM)
    q = jnp.transpose(qkv[:, :, 0], (0, 2, 1, 3))
    k = jnp.transpose(qkv[:, :, 1], (0, 2, 1, 3))
    v = jnp.transpose(qkv[:, :, 2], (0, 2, 1, 3))
    return q, k, v


def moba_prompt(x, w_qkv, w_o):
    bsz, seq, _ = x.shape
    q, k, v = _qkv(x, w_qkv)
    n_blocks = -(-seq // MOBA_BLOCK)
    pad = n_blocks * MOBA_BLOCK - seq
    k_pad = jnp.pad(k, ((0, 0), (0, 0), (0, pad), (0, 0)))
    v_pad = jnp.pad(v, ((0, 0), (0, 0), (0, pad), (0, 0)))
    k_blk = k_pad.reshape(bsz, N_HEADS, n_blocks, MOBA_BLOCK, HEAD_DIM)
    v_blk = v_pad.reshape(bsz, N_HEADS, n_blocks, MOBA_BLOCK, HEAD_DIM)
    k_mean = jnp.mean(k_blk.astype(jnp.float32), axis=3)
    k_sel = min(MOBA_TOPK, n_blocks - 1)
    b_idx = jnp.arange(bsz)[:, None, None, None]
    h_idx = jnp.arange(N_HEADS)[None, :, None, None]
    blk_ids = jnp.arange(n_blocks)

    def attend(q_start):
        qb = lax.dynamic_slice_in_dim(q, q_start, Q_BLOCK, axis=2)
        q_pos = q_start + jnp.arange(Q_BLOCK)
        own = q_start // MOBA_BLOCK
        own_start = own * MOBA_BLOCK
        k_own = lax.dynamic_slice_in_dim(k_pad, own_start, MOBA_BLOCK, axis=2)
        v_own = lax.dynamic_slice_in_dim(v_pad, own_start, MOBA_BLOCK, axis=2)
        k_pos = own_start + jnp.arange(MOBA_BLOCK)
        s_own = jnp.einsum('bhqd,bhkd->bhqk', qb, k_own).astype(jnp.float32) * ATTN_SCALE
        s_own = jnp.where(k_pos[None, :] <= q_pos[:, None], s_own, -jnp.inf)
        if k_sel <= 0:
            return _moba_combine(s_own, v_own, None, None)
        gate = jnp.einsum('bhqd,bhnd->bhqn', qb.astype(jnp.float32), k_mean)
        gate = jnp.where(blk_ids < own, gate, -jnp.inf)
        _, sel = lax.top_k(gate, k_sel)
        valid = sel < own
        k_g = k_blk[b_idx, h_idx, sel]
        v_g = v_blk[b_idx, h_idx, sel].reshape(bsz, N_HEADS, Q_BLOCK, k_sel * MOBA_BLOCK, HEAD_DIM)
        s_past = jnp.einsum('bhqd,bhqnkd->bhqnk', qb, k_g).astype(jnp.float32) * ATTN_SCALE
        s_past = jnp.where(valid[..., None], s_past, -jnp.inf).reshape(bsz, N_HEADS, Q_BLOCK, k_sel * MOBA_BLOCK)
        return _moba_combine(s_own, v_own, s_past, v_g)

    starts = jnp.arange(seq // Q_BLOCK) * Q_BLOCK
    o = lax.map(attend, starts)
    o = jnp.transpose(o, (1, 0, 3, 2, 4)).reshape(bsz, seq, D_MODEL)
    return o @ w_o, k, v


def moba_sample(x, k_pool, v_pool, page_table, w_qkv, w_o):
    bd, t_new, _ = x.shape
    past = page_table.shape[1] * PAGE_SIZE
    ppb = MOBA_BLOCK // PAGE_SIZE
    q, k, v = _qkv(x, w_qkv)
    n_full = past // MOBA_BLOCK
    own_start = n_full * MOBA_BLOCK
    r = past - own_start
    q_pos = past + jnp.arange(t_new)
    own_pages = page_table[:, own_start // PAGE_SIZE: past // PAGE_SIZE]
    k_own = jnp.concatenate([jnp.transpose(k_pool[own_pages], (0, 2, 1, 3, 4)).reshape(bd, N_HEADS, r, HEAD_DIM), k], axis=2)
    v_own = jnp.concatenate([jnp.transpose(v_pool[own_pages], (0, 2, 1, 3, 4)).reshape(bd, N_HEADS, r, HEAD_DIM), v], axis=2)
    k_pos = own_start + jnp.arange(r + t_new)
    s_own = jnp.einsum('bhqd,bhkd->bhqk', q, k_own).astype(jnp.float32) * ATTN_SCALE
    s_own = jnp.where(k_pos[None, :] <= q_pos[:, None], s_own, -jnp.inf)
    k_sel = min(MOBA_TOPK, n_full)
    if k_sel <= 0:
        o = _moba_combine(s_own, v_own, None, None)
    else:
        full_pages = page_table[:, :n_full * ppb]
        page_mean = jnp.mean(k_pool[full_pages].astype(jnp.float32), axis=3)
        k_mean = jnp.transpose(page_mean.reshape(bd, n_full, ppb, N_HEADS, HEAD_DIM).mean(axis=2), (0, 2, 1, 3))
        gate = jnp.einsum('bhqd,bhnd->bhqn', q.astype(jnp.float32), k_mean)
        _, sel = lax.top_k(gate, k_sel)
        logical = sel[..., None] * ppb + jnp.arange(ppb)
        phys = page_table[jnp.arange(bd)[:, None, None, None, None], logical]
        h_idx = jnp.arange(N_HEADS)[None, :, None, None, None]
        k_g = k_pool[phys, h_idx].reshape(bd, N_HEADS, t_new, k_sel * MOBA_BLOCK, HEAD_DIM)
        v_g = v_pool[phys, h_idx].reshape(bd, N_HEADS, t_new, k_sel * MOBA_BLOCK, HEAD_DIM)
        s_past = jnp.einsum('bhqd,bhqkd->bhqk', q, k_g).astype(jnp.float32) * ATTN_SCALE
        o = _moba_combine(s_own, v_own, s_past, v_g)
    o = jnp.transpose(o, (0, 2, 1, 3)).reshape(bd, t_new, D_MODEL)
    return o @ w_o, k, v


def swiglu(x, w_gate, w_up, w_down):
    return (jax.nn.silu(x @ w_gate) * (x @ w_up)) @ w_down


def moe_swiglu(x, w_router, w_gate, w_up, w_down):
    logits = (x @ w_router).astype(jnp.float32)
    top_val, top_idx = lax.top_k(logits, TOP_K_EXPERTS)
    top_w = jax.nn.softmax(top_val, axis=-1)
    gates = jnp.sum(jax.nn.one_hot(top_idx, N_EXPERTS, dtype=jnp.float32) * top_w[..., None], axis=-2)
    y = jnp.zeros(x.shape, jnp.float32)
    for e in range(N_EXPERTS):
        y = y + gates[..., e:e + 1] * swiglu(x, w_gate[e], w_up[e], w_down[e]).astype(jnp.float32)
    return y.astype(x.dtype)


def setup_inputs(seed: int = 0) -> dict:
    key = jax.random.key(seed)
    ks = jax.random.split(key, 24)
    n_pages = PAST_LEN // PAGE_SIZE
    n_phys = int(math.ceil(POOL_FACTOR * DEC_BATCH * n_pages))
    na, nb = N_CHUNK_LAYERS, N_ATTN_LAYERS

    def nrm(k, shape, scale):
        return jax.random.normal(k, shape, jnp.float32) * scale

    page_table = jax.random.permutation(ks[4], n_phys)[:DEC_BATCH * n_pages].reshape(DEC_BATCH, n_pages).astype(jnp.int32)
    return {
        'x_prompt': nrm(ks[0], (BATCH, SEQ, D_MODEL), 1.0),
        'x_sample': nrm(ks[1], (DEC_BATCH, DEC_SEQ, D_MODEL), 1.0),
        'cache_k': nrm(ks[2], (nb, n_phys, N_HEADS, PAGE_SIZE, HEAD_DIM), 1.0),
        'cache_v': nrm(ks[3], (nb, n_phys, N_HEADS, PAGE_SIZE, HEAD_DIM), 1.0),
        'page_table': page_table,
        'a_w_in': nrm(ks[5], (na, D_MODEL, 2 * A_WIDTH), D_MODEL ** -0.5),
        'a_ln_g': 1.0 + nrm(ks[6], (na, A_WIDTH), 0.02),
        'a_ln_b': nrm(ks[7], (na, A_WIDTH), 0.02),
        'a_w_s': nrm(ks[8], (na, A_GROUPS, CHUNK, CHUNK), CHUNK ** -0.5),
        'a_b_s': 1.0 + nrm(ks[9], (na, A_GROUPS, CHUNK), 0.1),
        'a_w_out': nrm(ks[10], (na, A_WIDTH, D_MODEL), DEEPNORM_BETA * A_WIDTH ** -0.5),
        'b_w_qkv': nrm(ks[11], (nb, D_MODEL, 3 * D_MODEL), D_MODEL ** -0.5),
        'b_w_o': nrm(ks[12], (nb, D_MODEL, D_MODEL), DEEPNORM_BETA * D_MODEL ** -0.5),
        'ffn_w_gate': nrm(ks[13], (na, D_MODEL, D_FF), D_MODEL ** -0.5),
        'ffn_w_up': nrm(ks[14], (na, D_MODEL, D_FF), D_MODEL ** -0.5),
        'ffn_w_down': nrm(ks[15], (na, D_FF, D_MODEL), DEEPNORM_BETA * D_FF ** -0.5),
        'moe_w_router': nrm(ks[16], (nb, D_MODEL, N_EXPERTS), D_MODEL ** -0.5),
        'moe_w_gate': nrm(ks[17], (nb, N_EXPERTS, D_MODEL, D_FF_EXPERT), D_MODEL ** -0.5),
        'moe_w_up': nrm(ks[18], (nb, N_EXPERTS, D_MODEL, D_FF_EXPERT), D_MODEL ** -0.5),
        'moe_w_down': nrm(ks[19], (nb, N_EXPERTS, D_FF_EXPERT, D_MODEL), DEEPNORM_BETA * D_FF_EXPERT ** -0.5),
        'ln1_g': 1.0 + nrm(ks[20], (DEPTH, D_MODEL), 0.02),
        'ln1_b': nrm(ks[21], (DEPTH, D_MODEL), 0.02),
        'ln2_g': 1.0 + nrm(ks[22], (DEPTH, D_MODEL), 0.02),
        'ln2_b': nrm(ks[23], (DEPTH, D_MODEL), 0.02),
    }


def reference(x_prompt, x_sample, cache_k, cache_v, page_table, a_w_in, a_ln_g, a_ln_b, a_w_s, a_b_s, a_w_out,
              b_w_qkv, b_w_o, ffn_w_gate, ffn_w_up, ffn_w_down, moe_w_router, moe_w_gate, moe_w_up, moe_w_down,
              ln1_g, ln1_b, ln2_g, ln2_b):
    xp, xs = x_prompt, x_sample
    chunk_v_s, k_p, v_p, k_s, v_s = [], [], [], [], []
    for i in range(DEPTH):
        j = i // 2
        if i % 2 == 0:
            mp, _ = chunk_mlp(xp, a_w_in[j], a_ln_g[j], a_ln_b[j], a_w_s[j], a_b_s[j], a_w_out[j])
            ms, v_rows = chunk_mlp(xs, a_w_in[j], a_ln_g[j], a_ln_b[j], a_w_s[j], a_b_s[j], a_w_out[j])
            chunk_v_s.append(v_rows)
        else:
            mp, kp_new, vp_new = moba_prompt(xp, b_w_qkv[j], b_w_o[j])
            ms, ks_new, vs_new = moba_sample(xs, cache_k[j], cache_v[j], page_table, b_w_qkv[j], b_w_o[j])
            k_p.append(kp_new)
            v_p.append(vp_new)
            k_s.append(ks_new)
            v_s.append(vs_new)
        xp = layer_norm(DEEPNORM_ALPHA * xp + mp, ln1_g[i], ln1_b[i])
        xs = layer_norm(DEEPNORM_ALPHA * xs + ms, ln1_g[i], ln1_b[i])
        if i % 2 == 0:
            fp = swiglu(xp, ffn_w_gate[j], ffn_w_up[j], ffn_w_down[j])
            fs = swiglu(xs, ffn_w_gate[j], ffn_w_up[j], ffn_w_down[j])
        else:
            fp = moe_swiglu(xp, moe_w_router[j], moe_w_gate[j], moe_w_up[j], moe_w_down[j])
            fs = moe_swiglu(xs, moe_w_router[j], moe_w_gate[j], moe_w_up[j], moe_w_down[j])
        xp = layer_norm(DEEPNORM_ALPHA * xp + fp, ln2_g[i], ln2_b[i])
        xs = layer_norm(DEEPNORM_ALPHA * xs + fs, ln2_g[i], ln2_b[i])
    return (xp, xs, jnp.stack(chunk_v_s), jnp.stack(k_p), jnp.stack(v_p), jnp.stack(k_s), jnp.stack(v_s))
```

```python
import functools

import jax
import jax.numpy as jnp
from jax import lax
from jax.experimental import pallas as pl
from jax.experimental.pallas import tpu as pltpu

F32 = jnp.float32
BF16 = jnp.bfloat16

D_MODEL = 1024
DEPTH = 2
CHUNK = 128
A_GROUPS = 8
N_HEADS = 8
HEAD_DIM = 128
MOBA_BLOCK = 256
MOBA_TOPK = 3
PAGE_SIZE = 128
N_EXPERTS = 8
ATTN_SCALE = HEAD_DIM ** -0.5
ALPHA = (2 * DEPTH) ** 0.25
LN_EPS = 1e-5
INV_SQRT2 = 0.7071067811865476
NEG_INF = float("-inf")

LANES = 128
MOE_TILE = 256
VMEM_LIMIT = 56 * 1024 * 1024

_NT_DIMS = (((1,), (1,)), ((), ()))


def _ln(x, g, b):
    mu = jnp.mean(x, axis=-1, keepdims=True)
    xc = x - mu
    var = jnp.mean(xc * xc, axis=-1, keepdims=True)
    return xc * lax.rsqrt(var + LN_EPS) * g + b


def _resident(shape):
    nd = len(shape)
    return pl.BlockSpec(shape, lambda *_: (0,) * nd, pipeline_mode=pl.Buffered(1))


def _params(sem, vmem=None):
    return pltpu.CompilerParams(dimension_semantics=sem, vmem_limit_bytes=vmem)


def _gmlp_kernel(x_ref, win_ref, lng_ref, lnb_ref, wmix_ref, bmix_ref, wout_ref, g1_ref, b1_ref,
                 *out_refs, n_chunks, emit_v):
    x = x_ref[...]
    h = jnp.dot(x.astype(BF16), win_ref[...], preferred_element_type=F32)
    h = 0.5 * h * (1.0 + lax.erf(h * INV_SQRT2))
    u = h[:, :D_MODEL]
    v = _ln(h[:, D_MODEL:], lng_ref[...], lnb_ref[...])
    if emit_v:
        out_refs[1][...] = v
    vb = v.astype(BF16)
    rows = []
    for c in range(n_chunks):
        cols = []
        for g in range(A_GROUPS):
            vg = vb[c * CHUNK:(c + 1) * CHUNK, g * LANES:(g + 1) * LANES]
            cols.append(jnp.dot(wmix_ref[g], vg, preferred_element_type=F32))
        rows.append(jnp.concatenate(cols, axis=1) + bmix_ref[...])
    z = rows[0] if n_chunks == 1 else jnp.concatenate(rows, axis=0)
    y = jnp.dot((u * z).astype(BF16), wout_ref[...], preferred_element_type=F32)
    out_refs[0][...] = _ln(ALPHA * x + y, g1_ref[...], b1_ref[...])


def _gmlp(x, w_in, ln_g, ln_b, wmix, bmix, w_out, g1, b1, *, tile, emit_v):
    rows = x.shape[0]
    n_chunks = tile // CHUNK
    row_spec = pl.BlockSpec((tile, D_MODEL), lambda i: (i, 0))
    out_shape = [jax.ShapeDtypeStruct((rows, D_MODEL), F32)]
    out_specs = [row_spec]
    if emit_v:
        out_shape.append(jax.ShapeDtypeStruct((rows, D_MODEL), F32))
        out_specs.append(row_spec)
    return pl.pallas_call(
        functools.partial(_gmlp_kernel, n_chunks=n_chunks, emit_v=emit_v),
        grid=(rows // tile,),
        in_specs=[row_spec, _resident(w_in.shape), _resident(ln_g.shape), _resident(ln_b.shape),
                  _resident(wmix.shape), _resident(bmix.shape), _resident(w_out.shape),
                  _resident(g1.shape), _resident(b1.shape)],
        out_specs=out_specs,
        out_shape=out_shape,
        compiler_params=_params(("parallel",), VMEM_LIMIT),
        name="gmlp_mixer",
    )(x, w_in, ln_g, ln_b, wmix, bmix, w_out, g1, b1)


def _swiglu_kernel(x_ref, wg_ref, wu_ref, wd_ref, g_ref, b_ref, o_ref):
    x = x_ref[...]
    xb = x.astype(BF16)
    a = jnp.dot(xb, wg_ref[...], preferred_element_type=F32)
    up = jnp.dot(xb, wu_ref[...], preferred_element_type=F32)
    h = (a * jax.nn.sigmoid(a)) * up
    f = jnp.dot(h.astype(BF16), wd_ref[...], preferred_element_type=F32)
    o_ref[...] = _ln(ALPHA * x + f, g_ref[...], b_ref[...])


def _swiglu(x, wg, wu, wd, g, b, *, tile):
    rows = x.shape[0]
    row_spec = pl.BlockSpec((tile, D_MODEL), lambda i: (i, 0))
    return pl.pallas_call(
        _swiglu_kernel,
        grid=(rows // tile,),
        in_specs=[row_spec, _resident(wg.shape), _resident(wu.shape), _resident(wd.shape),
                  _resident(g.shape), _resident(b.shape)],
        out_specs=row_spec,
        out_shape=jax.ShapeDtypeStruct((rows, D_MODEL), F32),
        compiler_params=_params(("parallel",), VMEM_LIMIT),
        name="dense_swiglu",
    )(x, wg, wu, wd, g, b)


def _qkv_prompt_kernel(x_ref, w_ref, qkv_ref, k_ref, v_ref, kmean_ref):
    r = jnp.dot(x_ref[...].astype(BF16), w_ref[...], preferred_element_type=F32)
    qkv_ref[...] = r.astype(BF16)
    for h in range(N_HEADS):
        k_ref[h] = r[:, D_MODEL + h * HEAD_DIM:D_MODEL + (h + 1) * HEAD_DIM]
        v_ref[h] = r[:, 2 * D_MODEL + h * HEAD_DIM:2 * D_MODEL + (h + 1) * HEAD_DIM]
    kmean_ref[...] = jnp.mean(r[:, D_MODEL:2 * D_MODEL], axis=0, keepdims=True)


def _qkv_prompt(x, w_qkv, bsz, seq):
    nb = seq // MOBA_BLOCK
    kv_shape = jax.ShapeDtypeStruct((bsz, N_HEADS, seq, HEAD_DIM), F32)
    kv_spec = pl.BlockSpec((None, N_HEADS, MOBA_BLOCK, HEAD_DIM), lambda i: (i // nb, 0, i % nb, 0))
    return pl.pallas_call(
        _qkv_prompt_kernel,
        grid=(bsz * nb,),
        in_specs=[pl.BlockSpec((MOBA_BLOCK, D_MODEL), lambda i: (i, 0)), _resident(w_qkv.shape)],
        out_specs=[pl.BlockSpec((MOBA_BLOCK, 3 * D_MODEL), lambda i: (i, 0)), kv_spec, kv_spec,
                   pl.BlockSpec((None, 1, D_MODEL), lambda i: (i, 0, 0))],
        out_shape=[jax.ShapeDtypeStruct((bsz * seq, 3 * D_MODEL), BF16), kv_shape, kv_shape,
                   jax.ShapeDtypeStruct((bsz * nb, 1, D_MODEL), F32)],
        compiler_params=_params(("parallel",), VMEM_LIMIT),
        name="qkv_prompt",
    )(x, w_qkv)


def _qkv_sample_kernel(x_ref, w_ref, o_ref):
    o_ref[...] = jnp.dot(x_ref[...].astype(BF16), w_ref[...], preferred_element_type=F32)


def _qkv_sample(x, w_qkv):
    rows = x.shape[0]
    return pl.pallas_call(
        _qkv_sample_kernel,
        grid=(1,),
        in_specs=[_resident(x.shape), _resident(w_qkv.shape)],
        out_specs=pl.BlockSpec((rows, 3 * D_MODEL), lambda i: (0, 0)),
        out_shape=jax.ShapeDtypeStruct((rows, 3 * D_MODEL), F32),
        compiler_params=_params(("arbitrary",), VMEM_LIMIT),
        name="qkv_sample",
    )(x, w_qkv)


def _first_argmax(g, col, width):
    m = jnp.max(g, axis=1, keepdims=True)
    return jnp.min(jnp.where(g == m, col, width), axis=1, keepdims=True)


def _moba_prompt_kernel(q_ref, k_ref, v_ref, km_ref, o_ref, sel_sc, m_sc, l_sc, acc_sc, *, n_blocks):
    qi = pl.program_id(2)
    q = q_ref[...]
    bq = q.shape[0]

    km = jnp.concatenate([km_ref[...], jnp.zeros((LANES - n_blocks, HEAD_DIM), F32)], axis=0)
    km_hi = km.astype(BF16)
    km_lo = (km - km_hi.astype(F32)).astype(BF16)
    gate = (lax.dot_general(q, km_hi, _NT_DIMS, preferred_element_type=F32)
            + lax.dot_general(q, km_lo, _NT_DIMS, preferred_element_type=F32))
    col = lax.broadcasted_iota(jnp.int32, (bq, LANES), 1)
    past = col < qi
    g = jnp.where(past, gate, NEG_INF)
    sel = jnp.zeros((bq, LANES), F32)
    for _ in range(MOBA_TOPK):
        pick = col == _first_argmax(g, col, LANES)
        sel = jnp.where(pick, 1.0, sel)
        g = jnp.where(pick, NEG_INF, g)
    sel_sc[...] = jnp.where(past, sel, 0.0)

    own = pl.multiple_of(qi * MOBA_BLOCK, MOBA_BLOCK)
    s = lax.dot_general(q, k_ref[pl.ds(own, MOBA_BLOCK), :], _NT_DIMS,
                        preferred_element_type=F32) * ATTN_SCALE
    r_id = lax.broadcasted_iota(jnp.int32, (bq, MOBA_BLOCK), 0)
    c_id = lax.broadcasted_iota(jnp.int32, (bq, MOBA_BLOCK), 1)
    s = jnp.where(c_id <= r_id, s, NEG_INF)
    m = jnp.max(s, axis=1, keepdims=True)
    p = jnp.exp(s - m)
    m_sc[...] = m
    l_sc[...] = jnp.sum(p, axis=1, keepdims=True)
    acc_sc[...] = jnp.dot(p.astype(BF16), v_ref[pl.ds(own, MOBA_BLOCK), :], preferred_element_type=F32)

    for j in range(n_blocks - 1):
        @pl.when(j < qi)
        def _():
            kj = k_ref[j * MOBA_BLOCK:(j + 1) * MOBA_BLOCK, :]
            vj = v_ref[j * MOBA_BLOCK:(j + 1) * MOBA_BLOCK, :]
            sj = lax.dot_general(q, kj, _NT_DIMS, preferred_element_type=F32) * ATTN_SCALE
            sj = jnp.where(sel_sc[:, j:j + 1] > 0.0, sj, NEG_INF)
            m_old = m_sc[...]
            m_new = jnp.maximum(m_old, jnp.max(sj, axis=1, keepdims=True))
            a = jnp.exp(m_old - m_new)
            pj = jnp.exp(sj - m_new)
            l_sc[...] = a * l_sc[...] + jnp.sum(pj, axis=1, keepdims=True)
            acc_sc[...] = a * acc_sc[...] + jnp.dot(pj.astype(BF16), vj, preferred_element_type=F32)
            m_sc[...] = m_new

    o_ref[...] = (acc_sc[...] / l_sc[...]).astype(o_ref.dtype)


def _moba_prompt(qkv, kmean, bsz, seq):
    nb = seq // MOBA_BLOCK
    return pl.pallas_call(
        functools.partial(_moba_prompt_kernel, n_blocks=nb),
        grid=(bsz, N_HEADS, nb),
        in_specs=[pl.BlockSpec((MOBA_BLOCK, HEAD_DIM), lambda b, h, i: (b * nb + i, h)),
                  pl.BlockSpec((seq, HEAD_DIM), lambda b, h, i: (b, N_HEADS + h)),
                  pl.BlockSpec((seq, HEAD_DIM), lambda b, h, i: (b, 2 * N_HEADS + h)),
                  pl.BlockSpec((None, nb, HEAD_DIM), lambda b, h, i: (b, 0, h))],
        out_specs=pl.BlockSpec((MOBA_BLOCK, HEAD_DIM), lambda b, h, i: (b * nb + i, h)),
        out_shape=jax.ShapeDtypeStruct((bsz * seq, D_MODEL), BF16),
        scratch_shapes=[pltpu.VMEM((MOBA_BLOCK, LANES), F32), pltpu.VMEM((MOBA_BLOCK, 1), F32),
                        pltpu.VMEM((MOBA_BLOCK, 1), F32), pltpu.VMEM((MOBA_BLOCK, HEAD_DIM), F32)],
        compiler_params=_params(("parallel", "parallel", "arbitrary"), VMEM_LIMIT),
        name="moba_prompt",
    )(qkv, qkv, qkv, kmean)


PAGES_PER_STEP = 4


def _ksum_kernel(pt_ref, p0, p1, p2, p3, o_ref):
    j = pl.program_id(1)

    @pl.when(j == 0)
    def _():
        o_ref[...] = jnp.zeros_like(o_ref)

    for half, (pa, pb) in enumerate(((p0, p1), (p2, p3))):
        s = jnp.sum(pa[...], axis=1) + jnp.sum(pb[...], axis=1)
        o_ref[:, pl.ds(2 * j + half, 1), :] = s[:, None, :]


def _ksum(page_table, k_pool):
    bd, n_pages = page_table.shape
    page = (None, N_HEADS, PAGE_SIZE, HEAD_DIM)

    def spec(m):
        return pl.BlockSpec(page, lambda b, j, pt: (pt[b, PAGES_PER_STEP * j + m], 0, 0, 0))

    return pl.pallas_call(
        _ksum_kernel,
        grid_spec=pltpu.PrefetchScalarGridSpec(
            num_scalar_prefetch=1,
            grid=(bd, n_pages // PAGES_PER_STEP),
            in_specs=[spec(m) for m in range(PAGES_PER_STEP)],
            out_specs=pl.BlockSpec((None, N_HEADS, LANES, HEAD_DIM), lambda b, j, pt: (b, 0, 0, 0)),
        ),
        out_shape=jax.ShapeDtypeStruct((bd, N_HEADS, LANES, HEAD_DIM), F32),
        compiler_params=_params(("parallel", "arbitrary"), VMEM_LIMIT),
        name="cache_block_sums",
    )(page_table, k_pool, k_pool, k_pool, k_pool)


def _split_bf16(a):
    hi = a.astype(BF16)
    return hi, (a - hi.astype(F32)).astype(BF16)


def _sample_select_kernel(q_ref, ks_ref, sel_ref, *, n_full):
    t_new = q_ref.shape[0]
    inv = 1.0 / MOBA_BLOCK
    gates = []
    for h in range(N_HEADS):
        qh = jnp.concatenate([q_ref[:, h * HEAD_DIM:(h + 1) * HEAD_DIM],
                              jnp.zeros((8 - t_new, HEAD_DIM), F32)], axis=0)
        q_hi, q_lo = _split_bf16(qh)
        k_hi, k_lo = _split_bf16(ks_ref[h] * inv)
        gates.append(lax.dot_general(q_hi, k_hi, _NT_DIMS, preferred_element_type=F32)
                     + lax.dot_general(q_hi, k_lo, _NT_DIMS, preferred_element_type=F32)
                     + lax.dot_general(q_lo, k_hi, _NT_DIMS, preferred_element_type=F32))
    g = jnp.concatenate(gates, axis=0)
    col = lax.broadcasted_iota(jnp.int32, g.shape, 1)
    g = jnp.where(col < n_full, g, NEG_INF)
    out = jnp.zeros(g.shape, jnp.int32)
    for r in range(MOBA_TOPK):
        idx = _first_argmax(g, col, LANES)
        out = jnp.where(col == r, idx, out)
        g = jnp.where(col == idx, NEG_INF, g)
    sel_ref[...] = out.reshape(sel_ref.shape)


def _sample_select(qkv_s, ksum, n_full):
    bd, t_new, _ = qkv_s.shape
    return pl.pallas_call(
        functools.partial(_sample_select_kernel, n_full=n_full),
        grid=(bd,),
        in_specs=[pl.BlockSpec((None, t_new, D_MODEL), lambda b: (b, 0, 0)),
                  pl.BlockSpec((None, N_HEADS, LANES, HEAD_DIM), lambda b: (b, 0, 0, 0))],
        out_specs=pl.BlockSpec((None, N_HEADS, 8, LANES), lambda b: (b, 0, 0, 0)),
        out_shape=jax.ShapeDtypeStruct((bd, N_HEADS, 8, LANES), jnp.int32),
        compiler_params=_params(("parallel",)),
        name="sample_select",
    )(qkv_s, ksum)


N_GATHER = MOBA_TOPK * (MOBA_BLOCK // PAGE_SIZE)


def _sample_attn_kernel(phys_ref, q_ref, kn_ref, vn_ref, *refs):
    k_pages = refs[:N_GATHER]
    v_pages = refs[N_GATHER:2 * N_GATHER]
    o_ref = refs[2 * N_GATHER]
    t = pl.program_id(2)
    t_new = kn_ref.shape[0]
    q = q_ref[pl.ds(t, 1), :]
    s_own = jnp.sum(kn_ref[...] * q, axis=1, keepdims=True) * ATTN_SCALE
    row = lax.broadcasted_iota(jnp.int32, (t_new, 1), 0)
    s_own = jnp.where(row <= t, s_own, NEG_INF)
    s_past = [jnp.sum(kp[...] * q, axis=1, keepdims=True) * ATTN_SCALE for kp in k_pages]
    m = jnp.max(s_own, axis=0, keepdims=True)
    for sp in s_past:
        m = jnp.maximum(m, jnp.max(sp, axis=0, keepdims=True))
    p_own = jnp.exp(s_own - m)
    l = jnp.sum(p_own, axis=0, keepdims=True)
    o = jnp.sum(p_own * vn_ref[...], axis=0, keepdims=True)
    for sp, vp in zip(s_past, v_pages):
        pp = jnp.exp(sp - m)
        l = l + jnp.sum(pp, axis=0, keepdims=True)
        o = o + jnp.sum(pp * vp[...], axis=0, keepdims=True)
    o_ref[pl.ds(t, 1), :] = o / l


def _sample_attn(phys, qkv_s, k_pool, v_pool):
    bd, t_new, _ = qkv_s.shape

    def page_spec(m):
        def imap(b, h, t, ph):
            return (ph[((b * N_HEADS + h) * t_new + t) * N_GATHER + m], h, 0, 0)
        return pl.BlockSpec((None, None, PAGE_SIZE, HEAD_DIM), imap)

    def head_spec(off):
        return pl.BlockSpec((None, t_new, HEAD_DIM), lambda b, h, t, ph: (b, 0, off + h))

    return pl.pallas_call(
        _sample_attn_kernel,
        grid_spec=pltpu.PrefetchScalarGridSpec(
            num_scalar_prefetch=1,
            grid=(bd, N_HEADS, t_new),
            in_specs=[head_spec(0), head_spec(N_HEADS), head_spec(2 * N_HEADS)]
                     + [page_spec(m) for m in range(N_GATHER)] * 2,
            out_specs=pl.BlockSpec((None, t_new, HEAD_DIM), lambda b, h, t, ph: (b, 0, h)),
        ),
        out_shape=jax.ShapeDtypeStruct((bd, t_new, D_MODEL), F32),
        compiler_params=_params(("parallel", "parallel", "arbitrary"), VMEM_LIMIT),
        name="sample_attn",
    )(phys, qkv_s, qkv_s, qkv_s, *([k_pool] * N_GATHER), *([v_pool] * N_GATHER))


def _attn_out_kernel(o_ref, x_ref, wo_ref, g_ref, b_ref, wr_ref, x3_ref, ti_ref, tw_ref):
    m = jnp.dot(o_ref[...].astype(BF16), wo_ref[...], preferred_element_type=F32)
    x3 = _ln(ALPHA * x_ref[...] + m, g_ref[...], b_ref[...])
    x3_ref[...] = x3
    logits = jnp.dot(x3.astype(BF16), wr_ref[...], preferred_element_type=F32)
    col = lax.broadcasted_iota(jnp.int32, logits.shape, 1)
    lg = jnp.where(col < N_EXPERTS, logits, NEG_INF)
    v1 = jnp.max(lg, axis=1, keepdims=True)
    i1 = jnp.min(jnp.where(lg == v1, col, LANES), axis=1, keepdims=True)
    lg2 = jnp.where(col == i1, NEG_INF, lg)
    v2 = jnp.max(lg2, axis=1, keepdims=True)
    i2 = jnp.min(jnp.where(lg2 == v2, col, LANES), axis=1, keepdims=True)
    e = jnp.exp(v2 - v1)
    w1 = 1.0 / (1.0 + e)
    w2 = e / (1.0 + e)
    ti_ref[...] = jnp.where(col == 0, i1, jnp.where(col == 1, i2, 0))
    tw_ref[...] = jnp.where(col == 0, w1, jnp.where(col == 1, w2, 0.0))


def _attn_out(o, x, w_o, g, b, w_r, *, tile):
    rows = x.shape[0]
    row_spec = pl.BlockSpec((tile, D_MODEL), lambda i: (i, 0))
    lane_spec = pl.BlockSpec((tile, LANES), lambda i: (i, 0))
    return pl.pallas_call(
        _attn_out_kernel,
        grid=(rows // tile,),
        in_specs=[row_spec, row_spec, _resident(w_o.shape), _resident(g.shape), _resident(b.shape),
                  _resident(w_r.shape)],
        out_specs=[row_spec, lane_spec, lane_spec],
        out_shape=[jax.ShapeDtypeStruct((rows, D_MODEL), F32),
                   jax.ShapeDtypeStruct((rows, LANES), jnp.int32),
                   jax.ShapeDtypeStruct((rows, LANES), F32)],
        compiler_params=_params(("parallel",), VMEM_LIMIT),
        name="attn_out_router",
    )(o, x, w_o, g, b, w_r)


def _gather_rows(idx_ref, src_hbm, dst, sem, n_rows):
    def issue(r, carry):
        pltpu.make_async_copy(src_hbm.at[pl.ds(idx_ref[r], 1)], dst.at[pl.ds(r, 1)], sem).start()
        return carry
    lax.fori_loop(0, n_rows, issue, 0)
    pltpu.make_async_copy(src_hbm.at[pl.ds(0, n_rows)], dst, sem).wait()


def _moe_kernel(te_ref, tv_ref, src_ref, x_hbm, ws_ref, wg_ref, wu_ref, wd_ref, y_ref, xbuf, sem):
    i = pl.program_id(0)

    @pl.when(tv_ref[i] == 1)
    def _():
        _gather_rows(src_ref, x_hbm, xbuf, sem, MOE_TILE)
        xb = xbuf[...].astype(BF16)
        a = jnp.dot(xb, wg_ref[...], preferred_element_type=F32)
        up = jnp.dot(xb, wu_ref[...], preferred_element_type=F32)
        h = (a * jax.nn.sigmoid(a)) * up
        f = jnp.dot(h.astype(BF16), wd_ref[...], preferred_element_type=F32)
        y_ref[...] = f * ws_ref[...]

    @pl.when(tv_ref[i] == 0)
    def _():
        y_ref[...] = jnp.zeros_like(y_ref)


def _moe(tile_expert, tile_valid, src, x3, w_sorted, wg, wu, wd):
    n_tiles = tile_expert.shape[0]
    d_ff = wg.shape[2]
    return pl.pallas_call(
        _moe_kernel,
        grid_spec=pltpu.PrefetchScalarGridSpec(
            num_scalar_prefetch=2,
            grid=(n_tiles,),
            in_specs=[pl.BlockSpec((MOE_TILE,), lambda i, te, tv: (i,), memory_space=pltpu.SMEM),
                      pl.BlockSpec(memory_space=pl.ANY),
                      pl.BlockSpec((MOE_TILE, 1), lambda i, te, tv: (i, 0)),
                      pl.BlockSpec((None, D_MODEL, d_ff), lambda i, te, tv: (te[i], 0, 0)),
                      pl.BlockSpec((None, D_MODEL, d_ff), lambda i, te, tv: (te[i], 0, 0)),
                      pl.BlockSpec((None, d_ff, D_MODEL), lambda i, te, tv: (te[i], 0, 0))],
            out_specs=pl.BlockSpec((MOE_TILE, D_MODEL), lambda i, te, tv: (i, 0)),
            scratch_shapes=[pltpu.VMEM((MOE_TILE, D_MODEL), F32), pltpu.SemaphoreType.DMA(())],
        ),
        out_shape=jax.ShapeDtypeStruct((n_tiles * MOE_TILE, D_MODEL), F32),
        compiler_params=_params(("arbitrary",), VMEM_LIMIT),
        name="moe_experts",
    )(tile_expert, tile_valid, src, x3, w_sorted, wg, wu, wd)


def _combine_kernel(pos_ref, y_hbm, x_ref, g_ref, b_ref, o_ref, ybuf, sem, *, tile):
    _gather_rows(pos_ref, y_hbm, ybuf, sem, 2 * tile)
    moe = ybuf[:tile, :] + ybuf[tile:, :]
    o_ref[...] = _ln(ALPHA * x_ref[...] + moe, g_ref[...], b_ref[...])


def _combine(pos, y_sorted, x3, g, b, *, tile):
    rows = x3.shape[0]
    row_spec = pl.BlockSpec((tile, D_MODEL), lambda i: (i, 0))
    return pl.pallas_call(
        functools.partial(_combine_kernel, tile=tile),
        grid=(rows // tile,),
        in_specs=[pl.BlockSpec((2 * tile,), lambda i: (i,), memory_space=pltpu.SMEM),
                  pl.BlockSpec(memory_space=pl.ANY), row_spec, _resident(g.shape), _resident(b.shape)],
        out_specs=row_spec,
        out_shape=jax.ShapeDtypeStruct((rows, D_MODEL), F32),
        scratch_shapes=[pltpu.VMEM((2 * tile, D_MODEL), F32), pltpu.SemaphoreType.DMA(())],
        compiler_params=_params(("arbitrary",), VMEM_LIMIT),
        name="moe_combine",
    )(pos, y_sorted, x3, g, b)


def _route(top_i, top_w, n_tiles, comb_tile):
    rows = top_i.shape[0]
    e_flat = top_i.reshape(-1)
    onehot = (e_flat[:, None] == jnp.arange(N_EXPERTS, dtype=jnp.int32)[None, :]).astype(jnp.int32)
    csum = jnp.cumsum(onehot, axis=0)
    rank = jnp.take_along_axis(csum, e_flat[:, None], axis=1)[:, 0] - 1
    tiles_e = (csum[-1] + MOE_TILE - 1) // MOE_TILE
    tile_end = jnp.cumsum(tiles_e)
    pos = (tile_end - tiles_e)[e_flat] * MOE_TILE + rank
    n_slots = n_tiles * MOE_TILE
    src = jnp.zeros((n_slots,), jnp.int32).at[pos].set(jnp.arange(2 * rows, dtype=jnp.int32) // 2)
    w_sorted = jnp.zeros((n_slots,), F32).at[pos].set(top_w.reshape(-1))
    tile_id = jnp.arange(n_tiles, dtype=jnp.int32)
    tile_expert = jnp.minimum(jnp.searchsorted(tile_end, tile_id, side="right"), N_EXPERTS - 1)
    tile_valid = (tile_id < tile_end[-1]).astype(jnp.int32)
    pos_tiles = pos.reshape(rows // comb_tile, comb_tile, 2).transpose(0, 2, 1).reshape(-1)
    return tile_expert.astype(jnp.int32), tile_valid, src, w_sorted[:, None], pos_tiles.astype(jnp.int32)


def kernel(x_prompt, x_sample, cache_k, cache_v, page_table, a_w_in, a_ln_g, a_ln_b, a_w_s, a_b_s, a_w_out,
           b_w_qkv, b_w_o, ffn_w_gate, ffn_w_up, ffn_w_down, moe_w_router, moe_w_gate, moe_w_up, moe_w_down,
           ln1_g, ln1_b, ln2_g, ln2_b):
    bsz, seq, _ = x_prompt.shape
    bd, t_new, _ = x_sample.shape
    rows_p, rows_s = bsz * seq, bd * t_new
    assert rows_s == CHUNK and CHUNK % t_new == 0 and seq % MOBA_BLOCK == 0
    n_pages = page_table.shape[1]
    past = n_pages * PAGE_SIZE
    assert past % MOBA_BLOCK == 0
    n_full = past // MOBA_BLOCK
    assert MOBA_TOPK <= n_full <= LANES and n_pages % PAGES_PER_STEP == 0

    xp = x_prompt.reshape(rows_p, D_MODEL)
    xs = x_sample.reshape(rows_s, D_MODEL)
    row = lambda a: a.reshape(1, -1)

    w_in = a_w_in[0].astype(BF16)
    w_out = a_w_out[0].astype(BF16)
    tril = jnp.tril(jnp.ones((CHUNK, CHUNK), bool))
    wmix_p = jnp.where(tril, a_w_s[0], 0.0).astype(BF16)
    bmix_p = jnp.repeat(a_b_s[0].T, LANES, axis=1)
    w_small = jnp.where(tril[:t_new, :t_new], a_w_s[0][:, :t_new, :t_new], 0.0)
    wmix_s = jnp.einsum("ab,gij->gaibj", jnp.eye(bd, dtype=F32), w_small).reshape(
        A_GROUPS, rows_s, rows_s).astype(BF16)
    bmix_s = jnp.repeat(jnp.tile(a_b_s[0][:, :t_new], (1, bd)).T, LANES, axis=1)
    gm_args = (w_in, row(a_ln_g[0]), row(a_ln_b[0]))
    gm_tail = (w_out, row(ln1_g[0]), row(ln1_b[0]))
    (xp,) = _gmlp(xp, *gm_args, wmix_p, bmix_p, *gm_tail, tile=512, emit_v=False)
    xs, v_rows = _gmlp(xs, *gm_args, wmix_s, bmix_s, *gm_tail, tile=rows_s, emit_v=True)

    ffn = (ffn_w_gate[0].astype(BF16), ffn_w_up[0].astype(BF16), ffn_w_down[0].astype(BF16),
           row(ln2_g[0]), row(ln2_b[0]))
    xp = _swiglu(xp, *ffn, tile=512)
    xs = _swiglu(xs, *ffn, tile=rows_s)

    w_qkv = b_w_qkv[0].astype(BF16)
    qkv_p, k_p, v_p, kmean = _qkv_prompt(xp, w_qkv, bsz, seq)
    o_p = _moba_prompt(qkv_p, kmean.reshape(bsz, seq // MOBA_BLOCK, D_MODEL), bsz, seq)

    qkv_s = _qkv_sample(xs, w_qkv).reshape(bd, t_new, 3 * D_MODEL)
    ksum = _ksum(page_table, cache_k[0])
    sel = _sample_select(qkv_s, ksum, n_full)[:, :, :t_new, :MOBA_TOPK]
    ppb = MOBA_BLOCK // PAGE_SIZE
    logical = (sel[..., None] * ppb + jnp.arange(ppb, dtype=jnp.int32)).reshape(bd, -1)
    phys = jnp.take_along_axis(page_table, logical, axis=1).reshape(-1)
    o_s = _sample_attn(phys, qkv_s, cache_k[0], cache_v[0]).reshape(rows_s, D_MODEL)
    kv_s = qkv_s.reshape(bd, t_new, 3, N_HEADS, HEAD_DIM)
    k_s = jnp.transpose(kv_s[:, :, 1], (0, 2, 1, 3))
    v_s = jnp.transpose(kv_s[:, :, 2], (0, 2, 1, 3))

    w_o = b_w_o[0].astype(BF16)
    w_r = jnp.pad(moe_w_router[0], ((0, 0), (0, LANES - N_EXPERTS))).astype(BF16)
    tail = (w_o, row(ln1_g[1]), row(ln1_b[1]), w_r)
    x3_p, ti_p, tw_p = _attn_out(o_p, xp, *tail, tile=512)
    x3_s, ti_s, tw_s = _attn_out(o_s, xs, *tail, tile=rows_s)

    x3 = jnp.concatenate([x3_p, x3_s], axis=0)
    top_i = jnp.concatenate([ti_p[:, :2], ti_s[:, :2]], axis=0)
    top_w = jnp.concatenate([tw_p[:, :2], tw_s[:, :2]], axis=0)
    rows = rows_p + rows_s
    n_tiles = (2 * rows) // MOE_TILE + N_EXPERTS
    comb_tile = CHUNK
    tile_expert, tile_valid, src, w_sorted, pos_tiles = _route(top_i, top_w, n_tiles, comb_tile)
    y_sorted = _moe(tile_expert, tile_valid, src, x3, w_sorted, moe_w_gate[0].astype(BF16),
                    moe_w_up[0].astype(BF16), moe_w_down[0].astype(BF16))
    x4 = _combine(pos_tiles, y_sorted, x3, row(ln2_g[1]), row(ln2_b[1]), tile=comb_tile)

    y_prompt = x4[:rows_p].reshape(bsz, seq, D_MODEL)
    y_sample = x4[rows_p:].reshape(bd, t_new, D_MODEL)
    return (y_prompt, y_sample, v_rows.reshape(1, bd, t_new, D_MODEL), k_p[None], v_p[None],
            k_s[None], v_s[None])
```

```python
import functools

import jax
import jax.numpy as jnp
from jax import lax
from jax.experimental import pallas as pl
from jax.experimental.pallas import tpu as pltpu

F32 = jnp.float32
BF16 = jnp.bfloat16

D_MODEL = 1024
DEPTH = 2
CHUNK = 128
A_GROUPS = 8
N_HEADS = 8
HEAD_DIM = 128
MOBA_BLOCK = 256
MOBA_TOPK = 3
PAGE_SIZE = 128
N_EXPERTS = 8
TOP_K_EXPERTS = 2
ATTN_SCALE = HEAD_DIM ** -0.5
EXP2_SCALE = ATTN_SCALE * 1.4426950408889634
ALPHA = (2 * DEPTH) ** 0.25
LN_EPS = 1e-5
INV_SQRT2 = 0.7071067811865476
NEG_INF = float("-inf")

LANES = 128
MOE_TILE = 256
ROW_TILE = 128
VMEM_LIMIT = 56 * 1024 * 1024

_NT_DIMS = (((1,), (1,)), ((), ()))


def _ln(x, g, b):
    mu = jnp.mean(x, axis=-1, keepdims=True)
    xc = x - mu
    var = jnp.mean(xc * xc, axis=-1, keepdims=True)
    return xc * lax.rsqrt(var + LN_EPS) * g + b


def _split_bf16(a):
    hi = a.astype(BF16)
    return hi, (a - hi.astype(F32)).astype(BF16)


def _resident(shape):
    nd = len(shape)
    return pl.BlockSpec(shape, lambda *_: (0,) * nd, pipeline_mode=pl.Buffered(1))


def _params(sem, vmem=None):
    return pltpu.CompilerParams(dimension_semantics=sem, vmem_limit_bytes=vmem)


def _gmlp_kernel(x_ref, win_ref, lng_ref, lnb_ref, wmix_ref, bmix_ref, wout_ref, g1_ref, b1_ref,
                 *out_refs, n_chunks, emit_v):
    x = x_ref[...]
    h = jnp.dot(x.astype(BF16), win_ref[...], preferred_element_type=F32)
    h = 0.5 * h * (1.0 + lax.erf(h * INV_SQRT2))
    u = h[:, :D_MODEL]
    v = _ln(h[:, D_MODEL:], lng_ref[...], lnb_ref[...])
    if emit_v:
        out_refs[1][...] = v
    vb = v.astype(BF16)
    rows = []
    for c in range(n_chunks):
        cols = []
        for g in range(A_GROUPS):
            vg = vb[c * CHUNK:(c + 1) * CHUNK, g * LANES:(g + 1) * LANES]
            cols.append(jnp.dot(wmix_ref[g], vg, preferred_element_type=F32))
        rows.append(jnp.concatenate(cols, axis=1) + bmix_ref[...])
    z = rows[0] if n_chunks == 1 else jnp.concatenate(rows, axis=0)
    y = jnp.dot((u * z).astype(BF16), wout_ref[...], preferred_element_type=F32)
    out_refs[0][...] = _ln(ALPHA * x + y, g1_ref[...], b1_ref[...])


def _gmlp(x, w_in, ln_g, ln_b, wmix, bmix, w_out, g1, b1, *, tile, emit_v):
    rows = x.shape[0]
    n_chunks = tile // CHUNK
    row_spec = pl.BlockSpec((tile, D_MODEL), lambda i: (i, 0))
    out_shape = [jax.ShapeDtypeStruct((rows, D_MODEL), F32)]
    out_specs = [row_spec]
    if emit_v:
        out_shape.append(jax.ShapeDtypeStruct((rows, D_MODEL), F32))
        out_specs.append(row_spec)
    return pl.pallas_call(
        functools.partial(_gmlp_kernel, n_chunks=n_chunks, emit_v=emit_v),
        grid=(rows // tile,),
        in_specs=[row_spec, _resident(w_in.shape), _resident(ln_g.shape), _resident(ln_b.shape),
                  _resident(wmix.shape), _resident(bmix.shape), _resident(w_out.shape),
                  _resident(g1.shape), _resident(b1.shape)],
        out_specs=out_specs,
        out_shape=out_shape,
        compiler_params=_params(("parallel",), VMEM_LIMIT),
        name="gmlp_mixer",
    )(x, w_in, ln_g, ln_b, wmix, bmix, w_out, g1, b1)


def _swiglu_body(xb, wg, wu, wd):
    a = jnp.dot(xb, wg, preferred_element_type=F32)
    up = jnp.dot(xb, wu, preferred_element_type=F32)
    h = (a * jax.nn.sigmoid(a)) * up
    return jnp.dot(h.astype(BF16), wd, preferred_element_type=F32)


def _swiglu_kernel(x_ref, wg_ref, wu_ref, wd_ref, g_ref, b_ref, o_ref):
    x = x_ref[...]
    f = _swiglu_body(x.astype(BF16), wg_ref[...], wu_ref[...], wd_ref[...])
    o_ref[...] = _ln(ALPHA * x + f, g_ref[...], b_ref[...])


def _swiglu(x, wg, wu, wd, g, b, *, tile):
    rows = x.shape[0]
    row_spec = pl.BlockSpec((tile, D_MODEL), lambda i: (i, 0))
    return pl.pallas_call(
        _swiglu_kernel,
        grid=(rows // tile,),
        in_specs=[row_spec, _resident(wg.shape), _resident(wu.shape), _resident(wd.shape),
                  _resident(g.shape), _resident(b.shape)],
        out_specs=row_spec,
        out_shape=jax.ShapeDtypeStruct((rows, D_MODEL), F32),
        compiler_params=_params(("parallel",), VMEM_LIMIT),
        name="dense_swiglu",
    )(x, wg, wu, wd, g, b)


def _qkv_prompt_kernel(x_ref, w_ref, qkv_ref, k_ref, v_ref, kmean_ref):
    r = jnp.dot(x_ref[...].astype(BF16), w_ref[...], preferred_element_type=F32)
    qkv_ref[...] = r.astype(BF16)
    for h in range(N_HEADS):
        k_ref[h] = r[:, D_MODEL + h * HEAD_DIM:D_MODEL + (h + 1) * HEAD_DIM]
        v_ref[h] = r[:, 2 * D_MODEL + h * HEAD_DIM:2 * D_MODEL + (h + 1) * HEAD_DIM]
    kmean_ref[...] = jnp.mean(r[:, D_MODEL:2 * D_MODEL], axis=0, keepdims=True)


def _qkv_prompt(x, w_qkv, bsz, seq):
    nb = seq // MOBA_BLOCK
    kv_shape = jax.ShapeDtypeStruct((bsz, N_HEADS, seq, HEAD_DIM), F32)
    kv_spec = pl.BlockSpec((None, N_HEADS, MOBA_BLOCK, HEAD_DIM), lambda i: (i // nb, 0, i % nb, 0))
    return pl.pallas_call(
        _qkv_prompt_kernel,
        grid=(bsz * nb,),
        in_specs=[pl.BlockSpec((MOBA_BLOCK, D_MODEL), lambda i: (i, 0)), _resident(w_qkv.shape)],
        out_specs=[pl.BlockSpec((MOBA_BLOCK, 3 * D_MODEL), lambda i: (i, 0)), kv_spec, kv_spec,
                   pl.BlockSpec((None, 1, D_MODEL), lambda i: (i, 0, 0))],
        out_shape=[jax.ShapeDtypeStruct((bsz * seq, 3 * D_MODEL), BF16), kv_shape, kv_shape,
                   jax.ShapeDtypeStruct((bsz * nb, 1, D_MODEL), F32)],
        compiler_params=_params(("parallel",), VMEM_LIMIT),
        name="qkv_prompt",
    )(x, w_qkv)


def _qkv_sample_kernel(x_ref, w_ref, o_ref):
    o_ref[...] = jnp.dot(x_ref[...].astype(BF16), w_ref[...], preferred_element_type=F32)


def _qkv_sample(x, w_qkv):
    rows = x.shape[0]
    return pl.pallas_call(
        _qkv_sample_kernel,
        grid=(1,),
        in_specs=[_resident(x.shape), _resident(w_qkv.shape)],
        out_specs=pl.BlockSpec((rows, 3 * D_MODEL), lambda i: (0, 0)),
        out_shape=jax.ShapeDtypeStruct((rows, 3 * D_MODEL), F32),
        compiler_params=_params(("arbitrary",), VMEM_LIMIT),
        name="qkv_sample",
    )(x, w_qkv)


PAIRS_PER_ROUND = 4


def _first_argmax(g, idx, size, axis):
    m = jnp.max(g, axis=axis, keepdims=True)
    return jnp.min(jnp.where(g == m, idx, size), axis=axis, keepdims=True)


def _moba_prompt_kernel(it_ref, jt_ref, q_ref, k_ref, v_ref, km_ref, o_ref,
                        vt_sc, sel_sc, m_sc, l_sc, acc_sc, sa_sc, sb_sc, *, n_blocks, n_own_rounds, n_rounds):
    nb = n_blocks
    km_hi, km_lo = _split_bf16(km_ref[...])

    def rows_of(block):
        return pl.ds(pl.multiple_of(block * MOBA_BLOCK, MOBA_BLOCK), MOBA_BLOCK)

    def transpose_v(j, carry):
        vt_sc[j] = v_ref[rows_of(j), :].astype(F32).T.astype(BF16)
        return carry
    lax.fori_loop(0, nb, transpose_v, 0, unroll=4)

    blk_id = lax.broadcasted_iota(jnp.int32, (nb, MOBA_BLOCK), 0)
    key_id = lax.broadcasted_iota(jnp.int32, (MOBA_BLOCK, MOBA_BLOCK), 0)
    qry_id = lax.broadcasted_iota(jnp.int32, (MOBA_BLOCK, MOBA_BLOCK), 1)

    def select_blocks(r, carry):
        tiles = [r * PAIRS_PER_ROUND + u for u in range(PAIRS_PER_ROUND)]
        gates = []
        for i in tiles:
            q = q_ref[rows_of(i), :]
            gates.append(lax.dot_general(km_hi, q, _NT_DIMS, preferred_element_type=F32)
                         + lax.dot_general(km_lo, q, _NT_DIMS, preferred_element_type=F32))
        for i, gate in zip(tiles, gates):
            past = blk_id < i
            g = jnp.where(past, gate, NEG_INF)
            sel = jnp.zeros((nb, MOBA_BLOCK), F32)
            for _ in range(MOBA_TOPK):
                pick = blk_id == _first_argmax(g, blk_id, nb, 0)
                sel = jnp.where(pick, 1.0, sel)
                g = jnp.where(pick, NEG_INF, g)
            sel = jnp.where(past, sel, 0.0)
            for j in range(nb):
                sel_sc[i, j] = sel[j:j + 1, :]
        return carry
    lax.fori_loop(0, n_own_rounds, select_blocks, 0)

    def pair_ids(c):
        return [(it_ref[c * PAIRS_PER_ROUND + u], jt_ref[c * PAIRS_PER_ROUND + u])
                for u in range(PAIRS_PER_ROUND)]

    def score_round(c, dst):
        for u, (i, j) in enumerate(pair_ids(c)):
            dst[u] = lax.dot_general(k_ref[rows_of(j), :], q_ref[rows_of(i), :], _NT_DIMS,
                                     preferred_element_type=F32)

    def softmax_own(c, src):
        ids = pair_ids(c)
        soft = []
        for u in range(PAIRS_PER_ROUND):
            s = jnp.where(key_id <= qry_id, src[u], NEG_INF)
            m = jnp.max(s, axis=0, keepdims=True)
            p = jnp.exp2((s - m) * EXP2_SCALE)
            soft.append((m, jnp.sum(p, axis=0, keepdims=True), p.astype(BF16)))
        pv = [jnp.dot(vt_sc[j], p, preferred_element_type=F32) for (_, j), (_, _, p) in zip(ids, soft)]
        for (i, _), (m, l, _), o in zip(ids, soft, pv):
            m_sc[i] = m
            l_sc[i] = l
            acc_sc[i] = o

    def softmax_past(c, src):
        ids = pair_ids(c)
        old = [(m_sc[i], l_sc[i], acc_sc[i]) for i, _ in ids]
        soft = []
        for u, ((i, j), (m_old, l_old, _)) in enumerate(zip(ids, old)):
            s = jnp.where(sel_sc[i, j] > 0.0, src[u], NEG_INF)
            m_new = jnp.maximum(m_old, jnp.max(s, axis=0, keepdims=True))
            a = jnp.exp2((m_old - m_new) * EXP2_SCALE)
            p = jnp.exp2((s - m_new) * EXP2_SCALE)
            soft.append((m_new, a, a * l_old + jnp.sum(p, axis=0, keepdims=True), p.astype(BF16)))
        pv = [jnp.dot(vt_sc[j], p, preferred_element_type=F32) for (_, j), (_, _, _, p) in zip(ids, soft)]
        for (i, _), (m_new, a, l_new, _), (_, _, acc_old), o in zip(ids, soft, old, pv):
            m_sc[i] = m_new
            l_sc[i] = l_new
            acc_sc[i] = a * acc_old + o

    def two_rounds(softmax_round):
        def body(h, carry):
            c = 2 * h
            score_round(c + 1, sb_sc)
            softmax_round(c, sa_sc)
            score_round(jnp.minimum(c + 2, n_rounds - 1), sa_sc)
            softmax_round(c + 1, sb_sc)
            return carry
        return body

    score_round(0, sa_sc)
    lax.fori_loop(0, n_own_rounds // 2, two_rounds(softmax_own), 0)
    lax.fori_loop(n_own_rounds // 2, n_rounds // 2, two_rounds(softmax_past), 0)

    def finish(i, carry):
        o_ref[rows_of(i), :] = (acc_sc[i] / l_sc[i]).T.astype(o_ref.dtype)
        return carry
    lax.fori_loop(0, nb, finish, 0, unroll=4)


def _pair_rounds(nb):
    assert nb % (2 * PAIRS_PER_ROUND) == 0
    rounds = [[(i, i) for i in range(r, r + PAIRS_PER_ROUND)] for r in range(0, nb, PAIRS_PER_ROUND)]
    n_own = len(rounds)
    remaining = {i: list(range(i)) for i in range(1, nb)}
    while any(remaining.values()):
        tiles = sorted((i for i in remaining if remaining[i]), key=lambda i: -len(remaining[i]))
        rounds.append([(i, remaining[i].pop()) for i in tiles[:PAIRS_PER_ROUND]])
    assert all(len(r) == PAIRS_PER_ROUND for r in rounds) and len(rounds) % 2 == 0
    return rounds, n_own


def _moba_prompt(qkv, kmean, bsz, seq):
    nb = seq // MOBA_BLOCK
    rounds, n_own = _pair_rounds(nb)
    i_tab = jnp.array([i for r in rounds for i, _ in r], jnp.int32)
    j_tab = jnp.array([j for r in rounds for _, j in r], jnp.int32)

    def head_spec(off):
        return pl.BlockSpec((seq, HEAD_DIM), lambda b, h, it, jt: (b, off + h))

    return pl.pallas_call(
        functools.partial(_moba_prompt_kernel, n_blocks=nb, n_own_rounds=n_own, n_rounds=len(rounds)),
        grid_spec=pltpu.PrefetchScalarGridSpec(
            num_scalar_prefetch=2,
            grid=(bsz, N_HEADS),
            in_specs=[head_spec(0), head_spec(N_HEADS), head_spec(2 * N_HEADS),
                      pl.BlockSpec((None, nb, HEAD_DIM), lambda b, h, it, jt: (b, 0, h))],
            out_specs=head_spec(0),
            scratch_shapes=[pltpu.VMEM((nb, HEAD_DIM, MOBA_BLOCK), BF16),
                            pltpu.VMEM((nb, nb, 1, MOBA_BLOCK), F32),
                            pltpu.VMEM((nb, 1, MOBA_BLOCK), F32),
                            pltpu.VMEM((nb, 1, MOBA_BLOCK), F32),
                            pltpu.VMEM((nb, HEAD_DIM, MOBA_BLOCK), F32),
                            pltpu.VMEM((PAIRS_PER_ROUND, MOBA_BLOCK, MOBA_BLOCK), F32),
                            pltpu.VMEM((PAIRS_PER_ROUND, MOBA_BLOCK, MOBA_BLOCK), F32)],
        ),
        out_shape=jax.ShapeDtypeStruct((bsz * seq, D_MODEL), BF16),
        compiler_params=_params(("parallel", "parallel"), VMEM_LIMIT),
        name="moba_prompt",
    )(i_tab, j_tab, qkv, qkv, qkv, kmean)


PAGES_PER_STEP = 8
PAGES_PER_BLOCK = MOBA_BLOCK // PAGE_SIZE


def _ksum_kernel(pt_ref, *refs):
    pages, o_ref = refs[:PAGES_PER_STEP], refs[PAGES_PER_STEP]
    j = pl.program_id(1)

    @pl.when(j == 0)
    def _():
        o_ref[...] = jnp.zeros_like(o_ref)

    blocks_per_step = PAGES_PER_STEP // PAGES_PER_BLOCK
    for n in range(blocks_per_step):
        s = jnp.sum(pages[PAGES_PER_BLOCK * n][...], axis=1)
        for m in range(1, PAGES_PER_BLOCK):
            s = s + jnp.sum(pages[PAGES_PER_BLOCK * n + m][...], axis=1)
        o_ref[:, pl.ds(blocks_per_step * j + n, 1), :] = s[:, None, :]


def _ksum(page_table, k_pool):
    bd, n_pages = page_table.shape
    page = (None, N_HEADS, PAGE_SIZE, HEAD_DIM)

    def spec(m):
        return pl.BlockSpec(page, lambda b, j, pt: (pt[b, PAGES_PER_STEP * j + m], 0, 0, 0))

    return pl.pallas_call(
        _ksum_kernel,
        grid_spec=pltpu.PrefetchScalarGridSpec(
            num_scalar_prefetch=1,
            grid=(bd, n_pages // PAGES_PER_STEP),
            in_specs=[spec(m) for m in range(PAGES_PER_STEP)],
            out_specs=pl.BlockSpec((None, N_HEADS, LANES, HEAD_DIM), lambda b, j, pt: (b, 0, 0, 0)),
        ),
        out_shape=jax.ShapeDtypeStruct((bd, N_HEADS, LANES, HEAD_DIM), F32),
        compiler_params=_params(("parallel", "arbitrary"), VMEM_LIMIT),
        name="cache_block_sums",
    )(page_table, *([k_pool] * PAGES_PER_STEP))


def _sample_select_kernel(q_ref, ks_ref, sel_ref, *, n_full):
    t_new = q_ref.shape[0]
    inv = 1.0 / MOBA_BLOCK
    gates = []
    for h in range(N_HEADS):
        qh = jnp.concatenate([q_ref[:, h * HEAD_DIM:(h + 1) * HEAD_DIM],
                              jnp.zeros((8 - t_new, HEAD_DIM), F32)], axis=0)
        q_hi, q_lo = _split_bf16(qh)
        k_hi, k_lo = _split_bf16(ks_ref[h] * inv)
        gates.append(lax.dot_general(q_hi, k_hi, _NT_DIMS, preferred_element_type=F32)
                     + lax.dot_general(q_hi, k_lo, _NT_DIMS, preferred_element_type=F32)
                     + lax.dot_general(q_lo, k_hi, _NT_DIMS, preferred_element_type=F32))
    g = jnp.concatenate(gates, axis=0)
    col = lax.broadcasted_iota(jnp.int32, g.shape, 1)
    g = jnp.where(col < n_full, g, NEG_INF)
    out = jnp.zeros(g.shape, jnp.int32)
    for r in range(MOBA_TOPK):
        idx = _first_argmax(g, col, LANES, 1)
        out = jnp.where(col == r, idx, out)
        g = jnp.where(col == idx, NEG_INF, g)
    sel_ref[...] = out.reshape(sel_ref.shape)


def _sample_select(qkv_s, ksum, n_full):
    bd, t_new, _ = qkv_s.shape
    return pl.pallas_call(
        functools.partial(_sample_select_kernel, n_full=n_full),
        grid=(bd,),
        in_specs=[pl.BlockSpec((None, t_new, D_MODEL), lambda b: (b, 0, 0)),
                  pl.BlockSpec((None, N_HEADS, LANES, HEAD_DIM), lambda b: (b, 0, 0, 0))],
        out_specs=pl.BlockSpec((None, N_HEADS, 8, LANES), lambda b: (b, 0, 0, 0)),
        out_shape=jax.ShapeDtypeStruct((bd, N_HEADS, 8, LANES), jnp.int32),
        compiler_params=_params(("parallel",)),
        name="sample_select",
    )(qkv_s, ksum)


N_GATHER = MOBA_TOPK * PAGES_PER_BLOCK


def _sample_attn_kernel(phys_ref, q_ref, kn_ref, vn_ref, kpool, vpool, o_ref, kbuf, vbuf, sem,
                        *, t_new, n_steps):
    n_in = t_new * N_GATHER
    step = pl.program_id(0) * N_HEADS + pl.program_id(1)
    slot = lax.rem(step, 2)

    def page_copies(st, sl):
        head = lax.rem(st, N_HEADS)
        out = []
        for m in range(n_in):
            page = phys_ref[st * n_in + m]
            out.append(pltpu.make_async_copy(kpool.at[page, head], kbuf.at[sl, m], sem.at[sl]))
            out.append(pltpu.make_async_copy(vpool.at[page, head], vbuf.at[sl, m], sem.at[sl]))
        return out

    @pl.when(step == 0)
    def _():
        for c in page_copies(step, slot):
            c.start()

    @pl.when(step + 1 < n_steps)
    def _():
        for c in page_copies(step + 1, 1 - slot):
            c.start()

    for c in page_copies(step, slot):
        c.wait()

    row = lax.broadcasted_iota(jnp.int32, (t_new, 1), 0)
    for t in range(t_new):
        q = q_ref[t:t + 1, :]
        pages = range(t * N_GATHER, (t + 1) * N_GATHER)
        s_own = jnp.sum(kn_ref[...] * q, axis=1, keepdims=True) * ATTN_SCALE
        s_own = jnp.where(row <= t, s_own, NEG_INF)
        s_past = [jnp.sum(kbuf[slot, m] * q, axis=1, keepdims=True) * ATTN_SCALE for m in pages]
        mx = jnp.max(s_own, axis=0, keepdims=True)
        for sp in s_past:
            mx = jnp.maximum(mx, jnp.max(sp, axis=0, keepdims=True))
        p_own = jnp.exp(s_own - mx)
        l = jnp.sum(p_own, axis=0, keepdims=True)
        o = jnp.sum(p_own * vn_ref[...], axis=0, keepdims=True)
        for sp, m in zip(s_past, pages):
            pp = jnp.exp(sp - mx)
            l = l + jnp.sum(pp, axis=0, keepdims=True)
            o = o + jnp.sum(pp * vbuf[slot, m], axis=0, keepdims=True)
        o_ref[t:t + 1, :] = o / l


def _sample_attn(phys, qkv_s, k_pool, v_pool):
    bd, t_new, _ = qkv_s.shape
    n_in = t_new * N_GATHER

    def head_spec(off):
        return pl.BlockSpec((None, t_new, HEAD_DIM), lambda b, h, ph: (b, 0, off + h))

    page_buf = pltpu.VMEM((2, n_in, PAGE_SIZE, HEAD_DIM), F32)
    return pl.pallas_call(
        functools.partial(_sample_attn_kernel, t_new=t_new, n_steps=bd * N_HEADS),
        grid_spec=pltpu.PrefetchScalarGridSpec(
            num_scalar_prefetch=1,
            grid=(bd, N_HEADS),
            in_specs=[head_spec(0), head_spec(N_HEADS), head_spec(2 * N_HEADS),
                      pl.BlockSpec(memory_space=pl.ANY), pl.BlockSpec(memory_space=pl.ANY)],
            out_specs=head_spec(0),
            scratch_shapes=[page_buf, page_buf, pltpu.SemaphoreType.DMA((2,))],
        ),
        out_shape=jax.ShapeDtypeStruct((bd, t_new, D_MODEL), F32),
        compiler_params=_params(("arbitrary", "arbitrary"), VMEM_LIMIT),
        name="sample_attn",
    )(phys, qkv_s, qkv_s, qkv_s, k_pool, v_pool)


def _attn_out_kernel(o_ref, x_ref, wo_ref, g_ref, b_ref, wr_ref, x3_ref, ti_ref, tw_ref):
    m = jnp.dot(o_ref[...].astype(BF16), wo_ref[...], preferred_element_type=F32)
    x3 = _ln(ALPHA * x_ref[...] + m, g_ref[...], b_ref[...])
    x3_ref[...] = x3
    logits = jnp.dot(x3.astype(BF16), wr_ref[...], preferred_element_type=F32)
    col = lax.broadcasted_iota(jnp.int32, logits.shape, 1)
    lg = jnp.where(col < N_EXPERTS, logits, NEG_INF)
    v1 = jnp.max(lg, axis=1, keepdims=True)
    i1 = jnp.min(jnp.where(lg == v1, col, LANES), axis=1, keepdims=True)
    lg2 = jnp.where(col == i1, NEG_INF, lg)
    v2 = jnp.max(lg2, axis=1, keepdims=True)
    i2 = jnp.min(jnp.where(lg2 == v2, col, LANES), axis=1, keepdims=True)
    e = jnp.exp(v2 - v1)
    w1 = 1.0 / (1.0 + e)
    w2 = e / (1.0 + e)
    ti_ref[...] = jnp.where(col == 0, i1, jnp.where(col == 1, i2, 0))
    tw_ref[...] = jnp.where(col == 0, w1, jnp.where(col == 1, w2, 0.0))


def _attn_out(o, x, w_o, g, b, w_r, *, tile):
    rows = x.shape[0]
    row_spec = pl.BlockSpec((tile, D_MODEL), lambda i: (i, 0))
    lane_spec = pl.BlockSpec((tile, LANES), lambda i: (i, 0))
    return pl.pallas_call(
        _attn_out_kernel,
        grid=(rows // tile,),
        in_specs=[row_spec, row_spec, _resident(w_o.shape), _resident(g.shape), _resident(b.shape),
                  _resident(w_r.shape)],
        out_specs=[row_spec, lane_spec, lane_spec],
        out_shape=[jax.ShapeDtypeStruct((rows, D_MODEL), F32),
                   jax.ShapeDtypeStruct((rows, LANES), jnp.int32),
                   jax.ShapeDtypeStruct((rows, LANES), F32)],
        compiler_params=_params(("parallel",), VMEM_LIMIT),
        name="attn_out_router",
    )(o, x, w_o, g, b, w_r)


def _dispatch_kernel(pos_ref, x_ref, xs_in, xs_out, sem):
    del xs_in
    n = x_ref.shape[0]

    def issue(r, carry):
        for k in range(TOP_K_EXPERTS):
            pltpu.make_async_copy(x_ref.at[pl.ds(r, 1)], xs_out.at[pl.ds(pos_ref[k * n + r], 1)], sem).start()
        return carry
    lax.fori_loop(0, n, issue, 0, unroll=8)
    for _ in range(TOP_K_EXPERTS):
        pltpu.make_async_copy(x_ref, xs_out.at[pl.ds(0, n)], sem).wait()


def _dispatch(pos, x, xs):
    rows = x.shape[0]
    return pl.pallas_call(
        _dispatch_kernel,
        grid=(rows // ROW_TILE,),
        in_specs=[pl.BlockSpec((TOP_K_EXPERTS * ROW_TILE,), lambda i: (i,), memory_space=pltpu.SMEM),
                  pl.BlockSpec((ROW_TILE, D_MODEL), lambda i: (i, 0)),
                  pl.BlockSpec(memory_space=pl.ANY)],
        out_specs=pl.BlockSpec(memory_space=pl.ANY),
        out_shape=jax.ShapeDtypeStruct(xs.shape, xs.dtype),
        scratch_shapes=[pltpu.SemaphoreType.DMA(())],
        input_output_aliases={2: 0},
        compiler_params=_params(("arbitrary",)),
        name="moe_dispatch",
    )(pos, x, xs)


def _moe_kernel(te_ref, tv_ref, xs_ref, wg_ref, wu_ref, wd_ref, y_ref):
    i = pl.program_id(0)

    @pl.when(tv_ref[i] == 1)
    def _():
        y_ref[...] = _swiglu_body(xs_ref[...].astype(BF16), wg_ref[...], wu_ref[...], wd_ref[...])

    @pl.when(tv_ref[i] == 0)
    def _():
        y_ref[...] = jnp.zeros_like(y_ref)


def _moe(tile_expert, tile_valid, xs, wg, wu, wd):
    n_tiles = tile_expert.shape[0]
    d_ff = wg.shape[2]
    tile_spec = pl.BlockSpec((MOE_TILE, D_MODEL), lambda i, te, tv: (i, 0))
    return pl.pallas_call(
        _moe_kernel,
        grid_spec=pltpu.PrefetchScalarGridSpec(
            num_scalar_prefetch=2,
            grid=(n_tiles,),
            in_specs=[tile_spec,
                      pl.BlockSpec((None, D_MODEL, d_ff), lambda i, te, tv: (te[i], 0, 0)),
                      pl.BlockSpec((None, D_MODEL, d_ff), lambda i, te, tv: (te[i], 0, 0)),
                      pl.BlockSpec((None, d_ff, D_MODEL), lambda i, te, tv: (te[i], 0, 0))],
            out_specs=tile_spec,
        ),
        out_shape=jax.ShapeDtypeStruct((n_tiles * MOE_TILE, D_MODEL), F32),
        compiler_params=_params(("arbitrary",), VMEM_LIMIT),
        name="moe_experts",
    )(tile_expert, tile_valid, xs, wg, wu, wd)


def _combine_kernel(pos_ref, y_hbm, x_ref, tw_ref, g_ref, b_ref, o_ref, ybuf, sem):
    n = x_ref.shape[0]
    n_rows = TOP_K_EXPERTS * n

    def issue(r, carry):
        pltpu.make_async_copy(y_hbm.at[pl.ds(pos_ref[r], 1)], ybuf.at[pl.ds(r, 1)], sem).start()
        return carry
    lax.fori_loop(0, n_rows, issue, 0, unroll=8)
    pltpu.make_async_copy(y_hbm.at[pl.ds(0, n_rows)], ybuf, sem).wait()
    tw = tw_ref[...]
    moe = tw[:, 0:1] * ybuf[:n, :] + tw[:, 1:2] * ybuf[n:, :]
    o_ref[...] = _ln(ALPHA * x_ref[...] + moe, g_ref[...], b_ref[...])


def _combine(pos, y_sorted, x3, tw, g, b):
    rows = x3.shape[0]
    row_spec = pl.BlockSpec((ROW_TILE, D_MODEL), lambda i: (i, 0))
    return pl.pallas_call(
        _combine_kernel,
        grid=(rows // ROW_TILE,),
        in_specs=[pl.BlockSpec((TOP_K_EXPERTS * ROW_TILE,), lambda i: (i,), memory_space=pltpu.SMEM),
                  pl.BlockSpec(memory_space=pl.ANY), row_spec,
                  pl.BlockSpec((ROW_TILE, LANES), lambda i: (i, 0)),
                  _resident(g.shape), _resident(b.shape)],
        out_specs=row_spec,
        out_shape=jax.ShapeDtypeStruct((rows, D_MODEL), F32),
        scratch_shapes=[pltpu.VMEM((TOP_K_EXPERTS * ROW_TILE, D_MODEL), F32), pltpu.SemaphoreType.DMA(())],
        compiler_params=_params(("arbitrary",), VMEM_LIMIT),
        name="moe_combine",
    )(pos, y_sorted, x3, tw, g, b)


def _route(top_i, n_tiles):
    rows = top_i.shape[0]
    e_flat = top_i.reshape(-1)
    experts = jnp.arange(N_EXPERTS, dtype=jnp.int32)
    onehot = (e_flat[:, None] == experts[None, :]).astype(jnp.int32)
    csum = jnp.cumsum(onehot, axis=0)
    rank = jnp.sum(onehot * csum, axis=1) - 1
    tiles_e = (csum[-1] + MOE_TILE - 1) // MOE_TILE
    tile_end = jnp.cumsum(tiles_e)
    tile_start = tile_end - tiles_e
    pos = jnp.sum(onehot * tile_start[None, :], axis=1) * MOE_TILE + rank
    tile_id = jnp.arange(n_tiles, dtype=jnp.int32)
    tile_expert = jnp.sum((tile_id[:, None] >= tile_end[None, :]).astype(jnp.int32), axis=1)
    tile_expert = jnp.minimum(tile_expert, N_EXPERTS - 1)
    tile_valid = (tile_id < tile_end[-1]).astype(jnp.int32)
    pos_tiles = pos.reshape(rows // ROW_TILE, ROW_TILE, TOP_K_EXPERTS).transpose(0, 2, 1).reshape(-1)
    return tile_expert, tile_valid, pos_tiles.astype(jnp.int32)


def kernel(x_prompt, x_sample, cache_k, cache_v, page_table, a_w_in, a_ln_g, a_ln_b, a_w_s, a_b_s, a_w_out,
           b_w_qkv, b_w_o, ffn_w_gate, ffn_w_up, ffn_w_down, moe_w_router, moe_w_gate, moe_w_up, moe_w_down,
           ln1_g, ln1_b, ln2_g, ln2_b):
    bsz, seq, _ = x_prompt.shape
    bd, t_new, _ = x_sample.shape
    rows_p, rows_s = bsz * seq, bd * t_new
    assert rows_s == CHUNK and CHUNK % t_new == 0 and seq % MOBA_BLOCK == 0
    n_pages = page_table.shape[1]
    past = n_pages * PAGE_SIZE
    assert past % MOBA_BLOCK == 0
    n_full = past // MOBA_BLOCK
    assert MOBA_TOPK <= n_full <= LANES and n_pages % PAGES_PER_STEP == 0

    xp = x_prompt.reshape(rows_p, D_MODEL)
    xs = x_sample.reshape(rows_s, D_MODEL)
    row = lambda a: a.reshape(1, -1)

    w_in = a_w_in[0].astype(BF16)
    w_out = a_w_out[0].astype(BF16)
    tril = jnp.tril(jnp.ones((CHUNK, CHUNK), bool))
    wmix_p = jnp.where(tril, a_w_s[0], 0.0).astype(BF16)
    bmix_p = jnp.repeat(a_b_s[0].T, LANES, axis=1)
    w_small = jnp.where(tril[:t_new, :t_new], a_w_s[0][:, :t_new, :t_new], 0.0)
    wmix_s = jnp.einsum("ab,gij->gaibj", jnp.eye(bd, dtype=F32), w_small).reshape(
        A_GROUPS, rows_s, rows_s).astype(BF16)
    bmix_s = jnp.repeat(jnp.tile(a_b_s[0][:, :t_new], (1, bd)).T, LANES, axis=1)
    gm_args = (w_in, row(a_ln_g[0]), row(a_ln_b[0]))
    gm_tail = (w_out, row(ln1_g[0]), row(ln1_b[0]))
    (xp,) = _gmlp(xp, *gm_args, wmix_p, bmix_p, *gm_tail, tile=512, emit_v=False)
    xs, v_rows = _gmlp(xs, *gm_args, wmix_s, bmix_s, *gm_tail, tile=rows_s, emit_v=True)

    ffn = (ffn_w_gate[0].astype(BF16), ffn_w_up[0].astype(BF16), ffn_w_down[0].astype(BF16),
           row(ln2_g[0]), row(ln2_b[0]))
    xp = _swiglu(xp, *ffn, tile=512)
    xs = _swiglu(xs, *ffn, tile=rows_s)

    w_qkv = b_w_qkv[0].astype(BF16)
    qkv_p, k_p, v_p, kmean = _qkv_prompt(xp, w_qkv, bsz, seq)
    o_p = _moba_prompt(qkv_p, kmean.reshape(bsz, seq // MOBA_BLOCK, D_MODEL), bsz, seq)

    qkv_s = _qkv_sample(xs, w_qkv).reshape(bd, t_new, 3 * D_MODEL)
    ksum = _ksum(page_table, cache_k[0])
    sel = _sample_select(qkv_s, ksum, n_full)[:, :, :t_new, :MOBA_TOPK]
    logical = (sel[..., None] * PAGES_PER_BLOCK
               + jnp.arange(PAGES_PER_BLOCK, dtype=jnp.int32)).reshape(bd, -1)
    phys = jnp.take_along_axis(page_table, logical, axis=1).reshape(-1)
    o_s = _sample_attn(phys, qkv_s, cache_k[0], cache_v[0]).reshape(rows_s, D_MODEL)
    kv_s = qkv_s.reshape(bd, t_new, 3, N_HEADS, HEAD_DIM)
    k_s = jnp.transpose(kv_s[:, :, 1], (0, 2, 1, 3))
    v_s = jnp.transpose(kv_s[:, :, 2], (0, 2, 1, 3))

    w_o = b_w_o[0].astype(BF16)
    w_r = jnp.pad(moe_w_router[0], ((0, 0), (0, LANES - N_EXPERTS))).astype(BF16)
    tail = (w_o, row(ln1_g[1]), row(ln1_b[1]), w_r)
    x3_p, ti_p, tw_p = _attn_out(o_p, xp, *tail, tile=512)
    x3_s, ti_s, tw_s = _attn_out(o_s, xs, *tail, tile=rows_s)

    top_i = jnp.concatenate([ti_p[:, :TOP_K_EXPERTS], ti_s[:, :TOP_K_EXPERTS]], axis=0)
    n_slots_used = TOP_K_EXPERTS * (rows_p + rows_s)
    n_tiles = n_slots_used // MOE_TILE + N_EXPERTS
    tile_expert, tile_valid, pos = _route(top_i, n_tiles)
    pos_p, pos_s = pos[:TOP_K_EXPERTS * rows_p], pos[TOP_K_EXPERTS * rows_p:]
    slots = jnp.zeros((n_tiles * MOE_TILE, D_MODEL), F32)
    slots = _dispatch(pos_p, x3_p, slots)
    slots = _dispatch(pos_s, x3_s, slots)
    y_slots = _moe(tile_expert, tile_valid, slots, moe_w_gate[0].astype(BF16),
                   moe_w_up[0].astype(BF16), moe_w_down[0].astype(BF16))
    ln2 = (row(ln2_g[1]), row(ln2_b[1]))
    y_prompt = _combine(pos_p, y_slots, x3_p, tw_p, *ln2).reshape(bsz, seq, D_MODEL)
    y_sample = _combine(pos_s, y_slots, x3_s, tw_s, *ln2).reshape(bd, t_new, D_MODEL)

    return (y_prompt, y_sample, v_rows.reshape(1, bd, t_new, D_MODEL), k_p[None], v_p[None],
            k_s[None], v_s[None])
```

```python
import functools

import jax
import jax.numpy as jnp
from jax import lax
from jax.experimental import pallas as pl
from jax.experimental.pallas import tpu as pltpu

F32 = jnp.float32
BF16 = jnp.bfloat16

D_MODEL = 1024
DEPTH = 2
CHUNK = 128
A_GROUPS = 8
N_HEADS = 8
HEAD_DIM = 128
MOBA_BLOCK = 256
MOBA_TOPK = 3
PAGE_SIZE = 128
PAGES_PER_BLOCK = MOBA_BLOCK // PAGE_SIZE
N_EXPERTS = 8
TOP_K_EXPERTS = 2
ATTN_SCALE = HEAD_DIM ** -0.5
EXP2_SCALE = ATTN_SCALE * 1.4426950408889634
ALPHA = (2 * DEPTH) ** 0.25
LN_EPS = 1e-5
INV_SQRT2 = 0.7071067811865476
NEG_INF = float("-inf")

LANES = 128
MOE_TILE = 256
ROW_TILE = 128
VMEM_LIMIT = 56 * 1024 * 1024

_NT_DIMS = (((1,), (1,)), ((), ()))


def _ln(x, g, b):
    mu = jnp.mean(x, axis=-1, keepdims=True)
    xc = x - mu
    var = jnp.mean(xc * xc, axis=-1, keepdims=True)
    return xc * lax.rsqrt(var + LN_EPS) * g + b


def _split_bf16(a):
    hi = a.astype(BF16)
    return hi, (a - hi.astype(F32)).astype(BF16)


def _resident(shape):
    nd = len(shape)
    return pl.BlockSpec(shape, lambda *_: (0,) * nd, pipeline_mode=pl.Buffered(1))


def _params(sem, vmem=None):
    return pltpu.CompilerParams(dimension_semantics=sem, vmem_limit_bytes=vmem)


def _gmlp_kernel(x_ref, win_ref, lng_ref, lnb_ref, wmix_ref, bmix_ref, wout_ref, g1_ref, b1_ref,
                 *out_refs, n_chunks, emit_v):
    x = x_ref[...]
    h = jnp.dot(x.astype(BF16), win_ref[...], preferred_element_type=F32)
    h = 0.5 * h * (1.0 + lax.erf(h * INV_SQRT2))
    u = h[:, :D_MODEL]
    v = _ln(h[:, D_MODEL:], lng_ref[...], lnb_ref[...])
    if emit_v:
        out_refs[1][...] = v
    vb = v.astype(BF16)
    rows = []
    for c in range(n_chunks):
        cols = []
        for g in range(A_GROUPS):
            vg = vb[c * CHUNK:(c + 1) * CHUNK, g * LANES:(g + 1) * LANES]
            cols.append(jnp.dot(wmix_ref[g], vg, preferred_element_type=F32))
        rows.append(jnp.concatenate(cols, axis=1) + bmix_ref[...])
    z = rows[0] if n_chunks == 1 else jnp.concatenate(rows, axis=0)
    y = jnp.dot((u * z).astype(BF16), wout_ref[...], preferred_element_type=F32)
    out_refs[0][...] = _ln(ALPHA * x + y, g1_ref[...], b1_ref[...])


def _gmlp(x, w_in, ln_g, ln_b, wmix, bmix, w_out, g1, b1, *, tile, emit_v):
    rows = x.shape[0]
    n_chunks = tile // CHUNK
    row_spec = pl.BlockSpec((tile, D_MODEL), lambda i: (i, 0))
    out_shape = [jax.ShapeDtypeStruct((rows, D_MODEL), F32)]
    out_specs = [row_spec]
    if emit_v:
        out_shape.append(jax.ShapeDtypeStruct((rows, D_MODEL), F32))
        out_specs.append(row_spec)
    return pl.pallas_call(
        functools.partial(_gmlp_kernel, n_chunks=n_chunks, emit_v=emit_v),
        grid=(rows // tile,),
        in_specs=[row_spec, _resident(w_in.shape), _resident(ln_g.shape), _resident(ln_b.shape),
                  _resident(wmix.shape), _resident(bmix.shape), _resident(w_out.shape),
                  _resident(g1.shape), _resident(b1.shape)],
        out_specs=out_specs,
        out_shape=out_shape,
        compiler_params=_params(("parallel",), VMEM_LIMIT),
        name="gmlp_mixer",
    )(x, w_in, ln_g, ln_b, wmix, bmix, w_out, g1, b1)


def _swiglu_body(xb, wg, wu, wd):
    a = jnp.dot(xb, wg, preferred_element_type=F32)
    up = jnp.dot(xb, wu, preferred_element_type=F32)
    h = (a * jax.nn.sigmoid(a)) * up
    return jnp.dot(h.astype(BF16), wd, preferred_element_type=F32)


def _swiglu_kernel(x_ref, wg_ref, wu_ref, wd_ref, g_ref, b_ref, o_ref):
    x = x_ref[...]
    f = _swiglu_body(x.astype(BF16), wg_ref[...], wu_ref[...], wd_ref[...])
    o_ref[...] = _ln(ALPHA * x + f, g_ref[...], b_ref[...])


def _swiglu(x, wg, wu, wd, g, b, *, tile):
    rows = x.shape[0]
    row_spec = pl.BlockSpec((tile, D_MODEL), lambda i: (i, 0))
    return pl.pallas_call(
        _swiglu_kernel,
        grid=(rows // tile,),
        in_specs=[row_spec, _resident(wg.shape), _resident(wu.shape), _resident(wd.shape),
                  _resident(g.shape), _resident(b.shape)],
        out_specs=row_spec,
        out_shape=jax.ShapeDtypeStruct((rows, D_MODEL), F32),
        compiler_params=_params(("parallel",), VMEM_LIMIT),
        name="dense_swiglu",
    )(x, wg, wu, wd, g, b)


def _qkv_prompt_kernel(x_ref, w_ref, qkv_ref, k_ref, v_ref, kmean_ref):
    r = jnp.dot(x_ref[...].astype(BF16), w_ref[...], preferred_element_type=F32)
    qkv_ref[...] = r.astype(BF16)
    for h in range(N_HEADS):
        k_ref[h] = r[:, D_MODEL + h * HEAD_DIM:D_MODEL + (h + 1) * HEAD_DIM]
        v_ref[h] = r[:, 2 * D_MODEL + h * HEAD_DIM:2 * D_MODEL + (h + 1) * HEAD_DIM]
    kmean_ref[...] = jnp.mean(r[:, D_MODEL:2 * D_MODEL], axis=0, keepdims=True)


def _qkv_prompt(x, w_qkv, bsz, seq):
    nb = seq // MOBA_BLOCK
    kv_shape = jax.ShapeDtypeStruct((bsz, N_HEADS, seq, HEAD_DIM), F32)
    kv_spec = pl.BlockSpec((None, N_HEADS, MOBA_BLOCK, HEAD_DIM), lambda i: (i // nb, 0, i % nb, 0))
    return pl.pallas_call(
        _qkv_prompt_kernel,
        grid=(bsz * nb,),
        in_specs=[pl.BlockSpec((MOBA_BLOCK, D_MODEL), lambda i: (i, 0)), _resident(w_qkv.shape)],
        out_specs=[pl.BlockSpec((MOBA_BLOCK, 3 * D_MODEL), lambda i: (i, 0)), kv_spec, kv_spec,
                   pl.BlockSpec((None, 1, D_MODEL), lambda i: (i, 0, 0))],
        out_shape=[jax.ShapeDtypeStruct((bsz * seq, 3 * D_MODEL), BF16), kv_shape, kv_shape,
                   jax.ShapeDtypeStruct((bsz * nb, 1, D_MODEL), F32)],
        compiler_params=_params(("parallel",), VMEM_LIMIT),
        name="qkv_prompt",
    )(x, w_qkv)


def _qkv_sample_kernel(x_ref, w_ref, o_ref):
    o_ref[...] = jnp.dot(x_ref[...].astype(BF16), w_ref[...], preferred_element_type=F32)


def _qkv_sample(x, w_qkv):
    rows = x.shape[0]
    return pl.pallas_call(
        _qkv_sample_kernel,
        grid=(1,),
        in_specs=[_resident(x.shape), _resident(w_qkv.shape)],
        out_specs=pl.BlockSpec((rows, 3 * D_MODEL), lambda i: (0, 0)),
        out_shape=jax.ShapeDtypeStruct((rows, 3 * D_MODEL), F32),
        compiler_params=_params(("arbitrary",), VMEM_LIMIT),
        name="qkv_sample",
    )(x, w_qkv)


PAIRS_PER_ROUND = 4


def _first_argmax(g, idx, size, axis):
    m = jnp.max(g, axis=axis, keepdims=True)
    return jnp.min(jnp.where(g == m, idx, size), axis=axis, keepdims=True)


def _moba_prompt_kernel(it_ref, jt_ref, pt_ref, q_ref, k_ref, v_ref, km_ref, kpool, o_ref, ks_ref,
                        vt_sc, sel_sc, m_sc, l_sc, acc_sc, sa_sc, sb_sc, pbuf, psem,
                        *, n_blocks, n_own_rounds, n_rounds, n_steps):
    nb = n_blocks
    km_hi, km_lo = _split_bf16(km_ref[...])

    step = pl.program_id(0) * N_HEADS + pl.program_id(1)
    n_past_trips = (n_rounds - n_own_rounds) // 2
    chunks_per_step = 1 + n_past_trips
    pages_per_chunk = pbuf.shape[1]
    blocks_per_chunk = pages_per_chunk // PAGES_PER_BLOCK
    last_chunk = n_steps * chunks_per_step - 1

    def chunk_copies(chunk, slot):
        seq = chunk // chunks_per_step
        first = (chunk - seq * chunks_per_step) * pages_per_chunk
        return [pltpu.make_async_copy(kpool.at[pt_ref[seq, first + m]], pbuf.at[slot, m], psem.at[slot])
                for m in range(pages_per_chunk)]

    def stream_chunk(c):
        chunk = step * chunks_per_step + c
        slot = lax.rem(c, 2)
        for cp in chunk_copies(chunk, slot):
            cp.wait()
        for cp in chunk_copies(jnp.minimum(chunk + 1, last_chunk), 1 - slot):
            cp.start()
        for n in range(blocks_per_chunk):
            tot = jnp.sum(pbuf[slot, PAGES_PER_BLOCK * n], axis=1)
            for m in range(1, PAGES_PER_BLOCK):
                tot = tot + jnp.sum(pbuf[slot, PAGES_PER_BLOCK * n + m], axis=1)
            ks_ref[:, pl.ds(c * blocks_per_chunk + n, 1), :] = tot[:, None, :]

    @pl.when(step == 0)
    def _():
        for cp in chunk_copies(0, 0):
            cp.start()

    ks_ref[...] = jnp.zeros_like(ks_ref)

    def rows_of(block):
        return pl.ds(pl.multiple_of(block * MOBA_BLOCK, MOBA_BLOCK), MOBA_BLOCK)

    def transpose_v(j, carry):
        vt_sc[j] = v_ref[rows_of(j), :].astype(F32).T.astype(BF16)
        return carry
    lax.fori_loop(0, nb, transpose_v, 0, unroll=4)

    blk_id = lax.broadcasted_iota(jnp.int32, (nb, MOBA_BLOCK), 0)
    key_id = lax.broadcasted_iota(jnp.int32, (MOBA_BLOCK, MOBA_BLOCK), 0)
    qry_id = lax.broadcasted_iota(jnp.int32, (MOBA_BLOCK, MOBA_BLOCK), 1)

    def select_blocks(r, carry):
        tiles = [r * PAIRS_PER_ROUND + u for u in range(PAIRS_PER_ROUND)]
        gates = []
        for i in tiles:
            q = q_ref[rows_of(i), :]
            gates.append(lax.dot_general(km_hi, q, _NT_DIMS, preferred_element_type=F32)
                         + lax.dot_general(km_lo, q, _NT_DIMS, preferred_element_type=F32))
        for i, gate in zip(tiles, gates):
            past = blk_id < i
            g = jnp.where(past, gate, NEG_INF)
            sel = jnp.zeros((nb, MOBA_BLOCK), F32)
            for _ in range(MOBA_TOPK):
                pick = blk_id == _first_argmax(g, blk_id, nb, 0)
                sel = jnp.where(pick, 1.0, sel)
                g = jnp.where(pick, NEG_INF, g)
            sel = jnp.where(past, sel, 0.0)
            for j in range(nb):
                sel_sc[i, j] = sel[j:j + 1, :]
        return carry
    lax.fori_loop(0, n_own_rounds, select_blocks, 0)

    def pair_ids(c):
        return [(it_ref[c * PAIRS_PER_ROUND + u], jt_ref[c * PAIRS_PER_ROUND + u])
                for u in range(PAIRS_PER_ROUND)]

    def score_round(c, dst):
        for u, (i, j) in enumerate(pair_ids(c)):
            dst[u] = lax.dot_general(k_ref[rows_of(j), :], q_ref[rows_of(i), :], _NT_DIMS,
                                     preferred_element_type=F32)

    def softmax_own(c, src):
        ids = pair_ids(c)
        soft = []
        for u in range(PAIRS_PER_ROUND):
            s = jnp.where(key_id <= qry_id, src[u], NEG_INF)
            m = jnp.max(s, axis=0, keepdims=True)
            p = jnp.exp2((s - m) * EXP2_SCALE)
            soft.append((m, jnp.sum(p, axis=0, keepdims=True), p.astype(BF16)))
        pv = [jnp.dot(vt_sc[j], p, preferred_element_type=F32) for (_, j), (_, _, p) in zip(ids, soft)]
        for (i, _), (m, l, _), o in zip(ids, soft, pv):
            m_sc[i] = m
            l_sc[i] = l
            acc_sc[i] = o

    def softmax_past(c, src):
        ids = pair_ids(c)
        old = [(m_sc[i], l_sc[i], acc_sc[i]) for i, _ in ids]
        soft = []
        for u, ((i, j), (m_old, l_old, _)) in enumerate(zip(ids, old)):
            s = jnp.where(sel_sc[i, j] > 0.0, src[u], NEG_INF)
            m_new = jnp.maximum(m_old, jnp.max(s, axis=0, keepdims=True))
            a = jnp.exp2((m_old - m_new) * EXP2_SCALE)
            p = jnp.exp2((s - m_new) * EXP2_SCALE)
            soft.append((m_new, a, a * l_old + jnp.sum(p, axis=0, keepdims=True), p.astype(BF16)))
        pv = [jnp.dot(vt_sc[j], p, preferred_element_type=F32) for (_, j), (_, _, _, p) in zip(ids, soft)]
        for (i, _), (m_new, a, l_new, _), (_, _, acc_old), o in zip(ids, soft, old, pv):
            m_sc[i] = m_new
            l_sc[i] = l_new
            acc_sc[i] = a * acc_old + o

    def two_rounds(softmax_round, stream):
        def body(h, carry):
            c = 2 * h
            score_round(c + 1, sb_sc)
            softmax_round(c, sa_sc)
            score_round(jnp.minimum(c + 2, n_rounds - 1), sa_sc)
            softmax_round(c + 1, sb_sc)
            if stream:
                stream_chunk(h - n_own_rounds // 2 + 1)
            return carry
        return body

    score_round(0, sa_sc)
    lax.fori_loop(0, n_own_rounds // 2, two_rounds(softmax_own, False), 0)
    stream_chunk(0)
    lax.fori_loop(n_own_rounds // 2, n_rounds // 2, two_rounds(softmax_past, True), 0)

    @pl.when(step == n_steps - 1)
    def _():
        for cp in chunk_copies(last_chunk, lax.rem(chunks_per_step, 2)):
            cp.wait()

    def finish(i, carry):
        o_ref[rows_of(i), :] = (acc_sc[i] / l_sc[i]).T.astype(o_ref.dtype)
        return carry
    lax.fori_loop(0, nb, finish, 0, unroll=4)


def _pair_rounds(nb):
    assert nb % (2 * PAIRS_PER_ROUND) == 0
    rounds = [[(i, i) for i in range(r, r + PAIRS_PER_ROUND)] for r in range(0, nb, PAIRS_PER_ROUND)]
    n_own = len(rounds)
    remaining = {i: list(range(i)) for i in range(1, nb)}
    while any(remaining.values()):
        tiles = sorted((i for i in remaining if remaining[i]), key=lambda i: -len(remaining[i]))
        rounds.append([(i, remaining[i].pop()) for i in tiles[:PAIRS_PER_ROUND]])
    assert all(len(r) == PAIRS_PER_ROUND for r in rounds) and len(rounds) % 2 == 0
    return rounds, n_own


def _moba_prompt(qkv, kmean, page_table, k_pool, bsz, seq):
    nb = seq // MOBA_BLOCK
    rounds, n_own = _pair_rounds(nb)
    i_tab = jnp.array([i for r in rounds for i, _ in r], jnp.int32)
    j_tab = jnp.array([j for r in rounds for _, j in r], jnp.int32)
    bd, n_pages = page_table.shape
    n_steps = bsz * N_HEADS
    chunks_per_step = 1 + (len(rounds) - n_own) // 2
    assert bd == n_steps and n_pages % (chunks_per_step * PAGES_PER_BLOCK) == 0
    pages_per_chunk = n_pages // chunks_per_step
    assert n_pages // PAGES_PER_BLOCK <= LANES

    def head_spec(off):
        return pl.BlockSpec((seq, HEAD_DIM), lambda b, h, *_: (b, off + h))

    return pl.pallas_call(
        functools.partial(_moba_prompt_kernel, n_blocks=nb, n_own_rounds=n_own, n_rounds=len(rounds),
                          n_steps=n_steps),
        grid_spec=pltpu.PrefetchScalarGridSpec(
            num_scalar_prefetch=3,
            grid=(bsz, N_HEADS),
            in_specs=[head_spec(0), head_spec(N_HEADS), head_spec(2 * N_HEADS),
                      pl.BlockSpec((None, nb, HEAD_DIM), lambda b, h, *_: (b, 0, h)),
                      pl.BlockSpec(memory_space=pl.ANY)],
            out_specs=[head_spec(0),
                       pl.BlockSpec((None, N_HEADS, LANES, HEAD_DIM),
                                    lambda b, h, *_: (b * N_HEADS + h, 0, 0, 0))],
            scratch_shapes=[pltpu.VMEM((nb, HEAD_DIM, MOBA_BLOCK), BF16),
                            pltpu.VMEM((nb, nb, 1, MOBA_BLOCK), F32),
                            pltpu.VMEM((nb, 1, MOBA_BLOCK), F32),
                            pltpu.VMEM((nb, 1, MOBA_BLOCK), F32),
                            pltpu.VMEM((nb, HEAD_DIM, MOBA_BLOCK), F32),
                            pltpu.VMEM((PAIRS_PER_ROUND, MOBA_BLOCK, MOBA_BLOCK), F32),
                            pltpu.VMEM((PAIRS_PER_ROUND, MOBA_BLOCK, MOBA_BLOCK), F32),
                            pltpu.VMEM((2, pages_per_chunk, N_HEADS, PAGE_SIZE, HEAD_DIM), F32),
                            pltpu.SemaphoreType.DMA((2,))],
        ),
        out_shape=[jax.ShapeDtypeStruct((bsz * seq, D_MODEL), BF16),
                   jax.ShapeDtypeStruct((bd, N_HEADS, LANES, HEAD_DIM), F32)],
        compiler_params=_params(("arbitrary", "arbitrary"), VMEM_LIMIT),
        name="moba_prompt",
    )(i_tab, j_tab, page_table, qkv, qkv, qkv, kmean, k_pool)


def _sample_select_kernel(q_ref, ks_ref, sel_ref, *, n_full):
    t_new = q_ref.shape[0]
    inv = 1.0 / MOBA_BLOCK
    gates = []
    for h in range(N_HEADS):
        qh = jnp.concatenate([q_ref[:, h * HEAD_DIM:(h + 1) * HEAD_DIM],
                              jnp.zeros((8 - t_new, HEAD_DIM), F32)], axis=0)
        q_hi, q_lo = _split_bf16(qh)
        k_hi, k_lo = _split_bf16(ks_ref[h] * inv)
        gates.append(lax.dot_general(q_hi, k_hi, _NT_DIMS, preferred_element_type=F32)
                     + lax.dot_general(q_hi, k_lo, _NT_DIMS, preferred_element_type=F32)
                     + lax.dot_general(q_lo, k_hi, _NT_DIMS, preferred_element_type=F32))
    g = jnp.concatenate(gates, axis=0)
    col = lax.broadcasted_iota(jnp.int32, g.shape, 1)
    g = jnp.where(col < n_full, g, NEG_INF)
    out = jnp.zeros(g.shape, jnp.int32)
    for r in range(MOBA_TOPK):
        idx = _first_argmax(g, col, LANES, 1)
        out = jnp.where(col == r, idx, out)
        g = jnp.where(col == idx, NEG_INF, g)
    sel_ref[...] = out.reshape(sel_ref.shape)


def _sample_select(qkv_s, ksum, n_full):
    bd, t_new, _ = qkv_s.shape
    return pl.pallas_call(
        functools.partial(_sample_select_kernel, n_full=n_full),
        grid=(bd,),
        in_specs=[pl.BlockSpec((None, t_new, D_MODEL), lambda b: (b, 0, 0)),
                  pl.BlockSpec((None, N_HEADS, LANES, HEAD_DIM), lambda b: (b, 0, 0, 0))],
        out_specs=pl.BlockSpec((None, N_HEADS, 8, LANES), lambda b: (b, 0, 0, 0)),
        out_shape=jax.ShapeDtypeStruct((bd, N_HEADS, 8, LANES), jnp.int32),
        compiler_params=_params(("parallel",)),
        name="sample_select",
    )(qkv_s, ksum)


N_GATHER = MOBA_TOPK * PAGES_PER_BLOCK


def _sample_attn_kernel(phys_ref, q_ref, kn_ref, vn_ref, kpool, vpool, o_ref, kbuf, vbuf, sem,
                        *, t_new, n_steps):
    n_in = t_new * N_GATHER
    n_slots = kbuf.shape[0]
    ahead = n_slots - 1
    step = pl.program_id(0) * N_HEADS + pl.program_id(1)
    slot = lax.rem(step, n_slots)

    def page_copies(st, sl):
        head = lax.rem(st, N_HEADS)
        out = []
        for m in range(n_in):
            page = phys_ref[st * n_in + m]
            out.append(pltpu.make_async_copy(kpool.at[page, head], kbuf.at[sl, m], sem.at[sl]))
            out.append(pltpu.make_async_copy(vpool.at[page, head], vbuf.at[sl, m], sem.at[sl]))
        return out

    @pl.when(step == 0)
    def _():
        for st in range(min(ahead, n_steps)):
            for c in page_copies(st, st):
                c.start()

    @pl.when(step + ahead < n_steps)
    def _():
        for c in page_copies(step + ahead, lax.rem(step + ahead, n_slots)):
            c.start()

    for c in page_copies(step, slot):
        c.wait()

    row = lax.broadcasted_iota(jnp.int32, (t_new, 1), 0)
    for t in range(t_new):
        q = q_ref[t:t + 1, :]
        pages = range(t * N_GATHER, (t + 1) * N_GATHER)
        s_own = jnp.sum(kn_ref[...] * q, axis=1, keepdims=True) * ATTN_SCALE
        s_own = jnp.where(row <= t, s_own, NEG_INF)
        s_past = [jnp.sum(kbuf[slot, m] * q, axis=1, keepdims=True) * ATTN_SCALE for m in pages]
        mx = jnp.max(s_own, axis=0, keepdims=True)
        for sp in s_past:
            mx = jnp.maximum(mx, jnp.max(sp, axis=0, keepdims=True))
        p_own = jnp.exp(s_own - mx)
        l = jnp.sum(p_own, axis=0, keepdims=True)
        o = jnp.sum(p_own * vn_ref[...], axis=0, keepdims=True)
        for sp, m in zip(s_past, pages):
            pp = jnp.exp(sp - mx)
            l = l + jnp.sum(pp, axis=0, keepdims=True)
            o = o + jnp.sum(pp * vbuf[slot, m], axis=0, keepdims=True)
        o_ref[t:t + 1, :] = o / l


def _sample_attn(phys, qkv_s, k_pool, v_pool):
    bd, t_new, _ = qkv_s.shape
    n_in = t_new * N_GATHER

    def head_spec(off):
        return pl.BlockSpec((None, t_new, HEAD_DIM), lambda b, h, ph: (b, 0, off + h))

    n_slots = 3
    page_buf = pltpu.VMEM((n_slots, n_in, PAGE_SIZE, HEAD_DIM), F32)
    return pl.pallas_call(
        functools.partial(_sample_attn_kernel, t_new=t_new, n_steps=bd * N_HEADS),
        grid_spec=pltpu.PrefetchScalarGridSpec(
            num_scalar_prefetch=1,
            grid=(bd, N_HEADS),
            in_specs=[head_spec(0), head_spec(N_HEADS), head_spec(2 * N_HEADS),
                      pl.BlockSpec(memory_space=pl.ANY), pl.BlockSpec(memory_space=pl.ANY)],
            out_specs=head_spec(0),
            scratch_shapes=[page_buf, page_buf, pltpu.SemaphoreType.DMA((n_slots,))],
        ),
        out_shape=jax.ShapeDtypeStruct((bd, t_new, D_MODEL), F32),
        compiler_params=_params(("arbitrary", "arbitrary"), VMEM_LIMIT),
        name="sample_attn",
    )(phys, qkv_s, qkv_s, qkv_s, k_pool, v_pool)


def _attn_out_kernel(o_ref, x_ref, wo_ref, g_ref, b_ref, wr_ref, x3_ref, ti_ref, tw_ref):
    m = jnp.dot(o_ref[...].astype(BF16), wo_ref[...], preferred_element_type=F32)
    x3 = _ln(ALPHA * x_ref[...] + m, g_ref[...], b_ref[...])
    x3_ref[...] = x3
    logits = jnp.dot(x3.astype(BF16), wr_ref[...], preferred_element_type=F32)
    col = lax.broadcasted_iota(jnp.int32, logits.shape, 1)
    lg = jnp.where(col < N_EXPERTS, logits, NEG_INF)
    v1 = jnp.max(lg, axis=1, keepdims=True)
    i1 = jnp.min(jnp.where(lg == v1, col, LANES), axis=1, keepdims=True)
    lg2 = jnp.where(col == i1, NEG_INF, lg)
    v2 = jnp.max(lg2, axis=1, keepdims=True)
    i2 = jnp.min(jnp.where(lg2 == v2, col, LANES), axis=1, keepdims=True)
    e = jnp.exp(v2 - v1)
    w1 = 1.0 / (1.0 + e)
    w2 = e / (1.0 + e)
    ti_ref[...] = jnp.where(col == 0, i1, jnp.where(col == 1, i2, 0))
    tw_ref[...] = jnp.where(col == 0, w1, jnp.where(col == 1, w2, 0.0))


def _attn_out(o, x, w_o, g, b, w_r, *, tile):
    rows = x.shape[0]
    row_spec = pl.BlockSpec((tile, D_MODEL), lambda i: (i, 0))
    lane_spec = pl.BlockSpec((tile, LANES), lambda i: (i, 0))
    return pl.pallas_call(
        _attn_out_kernel,
        grid=(rows // tile,),
        in_specs=[row_spec, row_spec, _resident(w_o.shape), _resident(g.shape), _resident(b.shape),
                  _resident(w_r.shape)],
        out_specs=[row_spec, lane_spec, lane_spec],
        out_shape=[jax.ShapeDtypeStruct((rows, D_MODEL), F32),
                   jax.ShapeDtypeStruct((rows, LANES), jnp.int32),
                   jax.ShapeDtypeStruct((rows, LANES), F32)],
        compiler_params=_params(("parallel",), VMEM_LIMIT),
        name="attn_out_router",
    )(o, x, w_o, g, b, w_r)


def _dispatch_kernel(pos_ref, x_hbm, xs_in, xs_out, sem, *, n_steps):
    del xs_in
    i = pl.program_id(0)
    slot = lax.rem(i, 2)
    n = ROW_TILE

    def issue(r, carry):
        for k in range(TOP_K_EXPERTS):
            pltpu.make_async_copy(x_hbm.at[pl.ds(i * n + r, 1)], xs_out.at[pl.ds(pos_ref[k * n + r], 1)],
                                  sem.at[slot]).start()
        return carry
    lax.fori_loop(0, n, issue, 0, unroll=8)

    def drain(sl):
        for _ in range(TOP_K_EXPERTS):
            pltpu.make_async_copy(x_hbm.at[pl.ds(0, n)], xs_out.at[pl.ds(0, n)], sem.at[sl]).wait()

    @pl.when(i > 0)
    def _():
        drain(1 - slot)

    @pl.when(i == n_steps - 1)
    def _():
        drain(slot)


def _dispatch(pos, x, xs):
    n_steps = x.shape[0] // ROW_TILE
    return pl.pallas_call(
        functools.partial(_dispatch_kernel, n_steps=n_steps),
        grid=(n_steps,),
        in_specs=[pl.BlockSpec((TOP_K_EXPERTS * ROW_TILE,), lambda i: (i,), memory_space=pltpu.SMEM),
                  pl.BlockSpec(memory_space=pl.ANY),
                  pl.BlockSpec(memory_space=pl.ANY)],
        out_specs=pl.BlockSpec(memory_space=pl.ANY),
        out_shape=jax.ShapeDtypeStruct(xs.shape, xs.dtype),
        scratch_shapes=[pltpu.SemaphoreType.DMA((2,))],
        input_output_aliases={2: 0},
        compiler_params=_params(("arbitrary",)),
        name="moe_dispatch",
    )(pos, x, xs)


def _moe_kernel(te_ref, tv_ref, xs_ref, wg_ref, wu_ref, wd_ref, y_ref):
    i = pl.program_id(0)

    @pl.when(tv_ref[i] == 1)
    def _():
        y_ref[...] = _swiglu_body(xs_ref[...].astype(BF16), wg_ref[...], wu_ref[...], wd_ref[...])

    @pl.when(tv_ref[i] == 0)
    def _():
        y_ref[...] = jnp.zeros_like(y_ref)


def _moe(tile_expert, tile_valid, xs, wg, wu, wd):
    n_tiles = tile_expert.shape[0]
    d_ff = wg.shape[2]
    tile_spec = pl.BlockSpec((MOE_TILE, D_MODEL), lambda i, te, tv: (i, 0))
    return pl.pallas_call(
        _moe_kernel,
        grid_spec=pltpu.PrefetchScalarGridSpec(
            num_scalar_prefetch=2,
            grid=(n_tiles,),
            in_specs=[tile_spec,
                      pl.BlockSpec((None, D_MODEL, d_ff), lambda i, te, tv: (te[i], 0, 0)),
                      pl.BlockSpec((None, D_MODEL, d_ff), lambda i, te, tv: (te[i], 0, 0)),
                      pl.BlockSpec((None, d_ff, D_MODEL), lambda i, te, tv: (te[i], 0, 0))],
            out_specs=tile_spec,
        ),
        out_shape=jax.ShapeDtypeStruct((n_tiles * MOE_TILE, D_MODEL), F32),
        compiler_params=_params(("arbitrary",), VMEM_LIMIT),
        name="moe_experts",
    )(tile_expert, tile_valid, xs, wg, wu, wd)


def _combine_kernel(pos_ref, pos_next_ref, y_hbm, x_ref, tw_ref, g_ref, b_ref, o_ref, ybuf, sem, *, n_steps):
    i = pl.program_id(0)
    slot = lax.rem(i, 2)
    n = x_ref.shape[0]
    n_rows = TOP_K_EXPERTS * n

    def gather(idx_ref, sl):
        for r in range(n_rows):
            pltpu.make_async_copy(y_hbm.at[pl.ds(idx_ref[r], 1)], ybuf.at[sl, pl.ds(r, 1)], sem.at[sl]).start()

    def wait_rows(sl):
        pltpu.make_async_copy(y_hbm.at[pl.ds(0, n_rows)], ybuf.at[sl], sem.at[sl]).wait()

    @pl.when(i == 0)
    def _():
        gather(pos_ref, slot)

    wait_rows(slot)
    gather(pos_next_ref, 1 - slot)
    tw = tw_ref[...]
    moe = tw[:, 0:1] * ybuf[slot, :n, :] + tw[:, 1:2] * ybuf[slot, n:, :]
    o_ref[...] = _ln(ALPHA * x_ref[...] + moe, g_ref[...], b_ref[...])

    @pl.when(i == n_steps - 1)
    def _():
        wait_rows(1 - slot)


def _combine(pos, y_sorted, x3, tw, g, b):
    rows = x3.shape[0]
    n_steps = rows // ROW_TILE
    row_spec = pl.BlockSpec((ROW_TILE, D_MODEL), lambda i: (i, 0))
    idx_block = (TOP_K_EXPERTS * ROW_TILE,)
    return pl.pallas_call(
        functools.partial(_combine_kernel, n_steps=n_steps),
        grid=(n_steps,),
        in_specs=[pl.BlockSpec(idx_block, lambda i: (i,), memory_space=pltpu.SMEM),
                  pl.BlockSpec(idx_block, lambda i: (jnp.minimum(i + 1, n_steps - 1),), memory_space=pltpu.SMEM),
                  pl.BlockSpec(memory_space=pl.ANY), row_spec,
                  pl.BlockSpec((ROW_TILE, LANES), lambda i: (i, 0)),
                  _resident(g.shape), _resident(b.shape)],
        out_specs=row_spec,
        out_shape=jax.ShapeDtypeStruct((rows, D_MODEL), F32),
        scratch_shapes=[pltpu.VMEM((2, TOP_K_EXPERTS * ROW_TILE, D_MODEL), F32), pltpu.SemaphoreType.DMA((2,))],
        compiler_params=_params(("arbitrary",), VMEM_LIMIT),
        name="moe_combine",
    )(pos, pos, y_sorted, x3, tw, g, b)


def _route(top_i, n_tiles):
    rows = top_i.shape[0]
    e_flat = top_i.reshape(-1)
    experts = jnp.arange(N_EXPERTS, dtype=jnp.int32)
    onehot = (e_flat[:, None] == experts[None, :]).astype(jnp.int32)
    csum = jnp.cumsum(onehot, axis=0)
    rank = jnp.sum(onehot * csum, axis=1) - 1
    tiles_e = (csum[-1] + MOE_TILE - 1) // MOE_TILE
    tile_end = jnp.cumsum(tiles_e)
    tile_start = tile_end - tiles_e
    pos = jnp.sum(onehot * tile_start[None, :], axis=1) * MOE_TILE + rank
    tile_id = jnp.arange(n_tiles, dtype=jnp.int32)
    tile_expert = jnp.sum((tile_id[:, None] >= tile_end[None, :]).astype(jnp.int32), axis=1)
    tile_expert = jnp.minimum(tile_expert, N_EXPERTS - 1)
    tile_valid = (tile_id < tile_end[-1]).astype(jnp.int32)
    pos_tiles = pos.reshape(rows // ROW_TILE, ROW_TILE, TOP_K_EXPERTS).transpose(0, 2, 1).reshape(-1)
    return tile_expert, tile_valid, pos_tiles.astype(jnp.int32)


def kernel(x_prompt, x_sample, cache_k, cache_v, page_table, a_w_in, a_ln_g, a_ln_b, a_w_s, a_b_s, a_w_out,
           b_w_qkv, b_w_o, ffn_w_gate, ffn_w_up, ffn_w_down, moe_w_router, moe_w_gate, moe_w_up, moe_w_down,
           ln1_g, ln1_b, ln2_g, ln2_b):
    bsz, seq, _ = x_prompt.shape
    bd, t_new, _ = x_sample.shape
    rows_p, rows_s = bsz * seq, bd * t_new
    assert rows_s == CHUNK and CHUNK % t_new == 0 and seq % MOBA_BLOCK == 0
    n_pages = page_table.shape[1]
    past = n_pages * PAGE_SIZE
    assert past % MOBA_BLOCK == 0
    n_full = past // MOBA_BLOCK
    assert MOBA_TOPK <= n_full <= LANES

    xp = x_prompt.reshape(rows_p, D_MODEL)
    xs = x_sample.reshape(rows_s, D_MODEL)
    row = lambda a: a.reshape(1, -1)

    w_in = a_w_in[0].astype(BF16)
    w_out = a_w_out[0].astype(BF16)
    tril = jnp.tril(jnp.ones((CHUNK, CHUNK), bool))
    wmix_p = jnp.where(tril, a_w_s[0], 0.0).astype(BF16)
    bmix_p = jnp.repeat(a_b_s[0].T, LANES, axis=1)
    w_small = jnp.where(tril[:t_new, :t_new], a_w_s[0][:, :t_new, :t_new], 0.0)
    wmix_s = jnp.einsum("ab,gij->gaibj", jnp.eye(bd, dtype=F32), w_small).reshape(
        A_GROUPS, rows_s, rows_s).astype(BF16)
    bmix_s = jnp.repeat(jnp.tile(a_b_s[0][:, :t_new], (1, bd)).T, LANES, axis=1)
    gm_args = (w_in, row(a_ln_g[0]), row(a_ln_b[0]))
    gm_tail = (w_out, row(ln1_g[0]), row(ln1_b[0]))
    (xp,) = _gmlp(xp, *gm_args, wmix_p, bmix_p, *gm_tail, tile=512, emit_v=False)
    xs, v_rows = _gmlp(xs, *gm_args, wmix_s, bmix_s, *gm_tail, tile=rows_s, emit_v=True)

    ffn = (ffn_w_gate[0].astype(BF16), ffn_w_up[0].astype(BF16), ffn_w_down[0].astype(BF16),
           row(ln2_g[0]), row(ln2_b[0]))
    xp = _swiglu(xp, *ffn, tile=512)
    xs = _swiglu(xs, *ffn, tile=rows_s)

    w_qkv = b_w_qkv[0].astype(BF16)
    qkv_p, k_p, v_p, kmean = _qkv_prompt(xp, w_qkv, bsz, seq)
    o_p, ksum = _moba_prompt(qkv_p, kmean.reshape(bsz, seq // MOBA_BLOCK, D_MODEL), page_table, cache_k[0],
                             bsz, seq)

    qkv_s = _qkv_sample(xs, w_qkv).reshape(bd, t_new, 3 * D_MODEL)
    sel = _sample_select(qkv_s, ksum, n_full)[:, :, :t_new, :MOBA_TOPK]
    logical = (sel[..., None] * PAGES_PER_BLOCK
               + jnp.arange(PAGES_PER_BLOCK, dtype=jnp.int32)).reshape(bd, -1)
    phys = jnp.take_along_axis(page_table, logical, axis=1).reshape(-1)
    o_s = _sample_attn(phys, qkv_s, cache_k[0], cache_v[0]).reshape(rows_s, D_MODEL)
    kv_s = qkv_s.reshape(bd, t_new, 3, N_HEADS, HEAD_DIM)
    k_s = jnp.transpose(kv_s[:, :, 1], (0, 2, 1, 3))
    v_s = jnp.transpose(kv_s[:, :, 2], (0, 2, 1, 3))

    w_o = b_w_o[0].astype(BF16)
    w_r = jnp.pad(moe_w_router[0], ((0, 0), (0, LANES - N_EXPERTS))).astype(BF16)
    tail = (w_o, row(ln1_g[1]), row(ln1_b[1]), w_r)
    x3_p, ti_p, tw_p = _attn_out(o_p, xp, *tail, tile=512)
    x3_s, ti_s, tw_s = _attn_out(o_s, xs, *tail, tile=rows_s)

    top_i = jnp.concatenate([ti_p[:, :TOP_K_EXPERTS], ti_s[:, :TOP_K_EXPERTS]], axis=0)
    n_slots_used = TOP_K_EXPERTS * (rows_p + rows_s)
    n_tiles = n_slots_used // MOE_TILE + N_EXPERTS
    tile_expert, tile_valid, pos = _route(top_i, n_tiles)
    pos_p, pos_s = pos[:TOP_K_EXPERTS * rows_p], pos[TOP_K_EXPERTS * rows_p:]
    slots = jnp.zeros((n_tiles * MOE_TILE, D_MODEL), F32)
    slots = _dispatch(pos_p, x3_p, slots)
    slots = _dispatch(pos_s, x3_s, slots)
    y_slots = _moe(tile_expert, tile_valid, slots, moe_w_gate[0].astype(BF16),
                   moe_w_up[0].astype(BF16), moe_w_down[0].astype(BF16))
    ln2 = (row(ln2_g[1]), row(ln2_b[1]))
    y_prompt = _combine(pos_p, y_slots, x3_p, tw_p, *ln2).reshape(bsz, seq, D_MODEL)
    y_sample = _combine(pos_s, y_slots, x3_s, tw_s, *ln2).reshape(bd, t_new, D_MODEL)

    return (y_prompt, y_sample, v_rows.reshape(1, bd, t_new, D_MODEL), k_p[None], v_p[None],
            k_s[None], v_s[None])
```

```python
import functools

import jax
import jax.numpy as jnp
from jax import lax
from jax.experimental import pallas as pl
from jax.experimental.pallas import tpu as pltpu

F32 = jnp.float32
BF16 = jnp.bfloat16

D_MODEL = 1024
DEPTH = 2
CHUNK = 128
A_GROUPS = 8
N_HEADS = 8
HEAD_DIM = 128
MOBA_BLOCK = 256
MOBA_TOPK = 3
PAGE_SIZE = 128
PAGES_PER_BLOCK = MOBA_BLOCK // PAGE_SIZE
N_EXPERTS = 8
TOP_K_EXPERTS = 2
ATTN_SCALE = HEAD_DIM ** -0.5
EXP2_SCALE = ATTN_SCALE * 1.4426950408889634
ALPHA = (2 * DEPTH) ** 0.25
LN_EPS = 1e-5
INV_SQRT2 = 0.7071067811865476
NEG_INF = float("-inf")

LANES = 128
MOE_TILE = 256
ROW_TILE = 128
VMEM_LIMIT = 56 * 1024 * 1024

_NT_DIMS = (((1,), (1,)), ((), ()))


def _ln(x, g, b):
    mu = jnp.mean(x, axis=-1, keepdims=True)
    xc = x - mu
    var = jnp.mean(xc * xc, axis=-1, keepdims=True)
    return xc * lax.rsqrt(var + LN_EPS) * g + b


def _split_bf16(a):
    hi = a.astype(BF16)
    return hi, (a - hi.astype(F32)).astype(BF16)


def _resident(shape):
    nd = len(shape)
    return pl.BlockSpec(shape, lambda *_: (0,) * nd, pipeline_mode=pl.Buffered(1))


def _params(sem, vmem=None):
    return pltpu.CompilerParams(dimension_semantics=sem, vmem_limit_bytes=vmem)


def _gmlp_kernel(x_ref, win_ref, lng_ref, lnb_ref, wmix_ref, bmix_ref, wout_ref, g1_ref, b1_ref,
                 *out_refs, n_chunks, emit_v):
    x = x_ref[...]
    h = jnp.dot(x.astype(BF16), win_ref[...], preferred_element_type=F32)
    h = 0.5 * h * (1.0 + lax.erf(h * INV_SQRT2))
    u = h[:, :D_MODEL]
    v = _ln(h[:, D_MODEL:], lng_ref[...], lnb_ref[...])
    if emit_v:
        out_refs[1][...] = v
    vb = v.astype(BF16)
    rows = []
    for c in range(n_chunks):
        cols = []
        for g in range(A_GROUPS):
            vg = vb[c * CHUNK:(c + 1) * CHUNK, g * LANES:(g + 1) * LANES]
            cols.append(jnp.dot(wmix_ref[g], vg, preferred_element_type=F32))
        rows.append(jnp.concatenate(cols, axis=1) + bmix_ref[...])
    z = rows[0] if n_chunks == 1 else jnp.concatenate(rows, axis=0)
    y = jnp.dot((u * z).astype(BF16), wout_ref[...], preferred_element_type=F32)
    out_refs[0][...] = _ln(ALPHA * x + y, g1_ref[...], b1_ref[...])


def _gmlp(x, w_in, ln_g, ln_b, wmix, bmix, w_out, g1, b1, *, tile, emit_v):
    rows = x.shape[0]
    n_chunks = tile // CHUNK
    row_spec = pl.BlockSpec((tile, D_MODEL), lambda i: (i, 0))
    out_shape = [jax.ShapeDtypeStruct((rows, D_MODEL), F32)]
    out_specs = [row_spec]
    if emit_v:
        out_shape.append(jax.ShapeDtypeStruct((rows, D_MODEL), F32))
        out_specs.append(row_spec)
    return pl.pallas_call(
        functools.partial(_gmlp_kernel, n_chunks=n_chunks, emit_v=emit_v),
        grid=(rows // tile,),
        in_specs=[row_spec, _resident(w_in.shape), _resident(ln_g.shape), _resident(ln_b.shape),
                  _resident(wmix.shape), _resident(bmix.shape), _resident(w_out.shape),
                  _resident(g1.shape), _resident(b1.shape)],
        out_specs=out_specs,
        out_shape=out_shape,
        compiler_params=_params(("parallel",), VMEM_LIMIT),
        name="gmlp_mixer",
    )(x, w_in, ln_g, ln_b, wmix, bmix, w_out, g1, b1)


def _swiglu_body(xb, wg, wu, wd):
    a = jnp.dot(xb, wg, preferred_element_type=F32)
    up = jnp.dot(xb, wu, preferred_element_type=F32)
    h = (a * jax.nn.sigmoid(a)) * up
    return jnp.dot(h.astype(BF16), wd, preferred_element_type=F32)


def _swiglu_kernel(x_ref, wg_ref, wu_ref, wd_ref, g_ref, b_ref, o_ref):
    x = x_ref[...]
    f = _swiglu_body(x.astype(BF16), wg_ref[...], wu_ref[...], wd_ref[...])
    o_ref[...] = _ln(ALPHA * x + f, g_ref[...], b_ref[...])


def _swiglu(x, wg, wu, wd, g, b, *, tile):
    rows = x.shape[0]
    row_spec = pl.BlockSpec((tile, D_MODEL), lambda i: (i, 0))
    return pl.pallas_call(
        _swiglu_kernel,
        grid=(rows // tile,),
        in_specs=[row_spec, _resident(wg.shape), _resident(wu.shape), _resident(wd.shape),
                  _resident(g.shape), _resident(b.shape)],
        out_specs=row_spec,
        out_shape=jax.ShapeDtypeStruct((rows, D_MODEL), F32),
        compiler_params=_params(("parallel",), VMEM_LIMIT),
        name="dense_swiglu",
    )(x, wg, wu, wd, g, b)


def _qkv_prompt_kernel(x_ref, w_ref, qkv_ref, k_ref, v_ref, kmean_ref):
    r = jnp.dot(x_ref[...].astype(BF16), w_ref[...], preferred_element_type=F32)
    qkv_ref[...] = r.astype(BF16)
    for h in range(N_HEADS):
        k_ref[h] = r[:, D_MODEL + h * HEAD_DIM:D_MODEL + (h + 1) * HEAD_DIM]
        v_ref[h] = r[:, 2 * D_MODEL + h * HEAD_DIM:2 * D_MODEL + (h + 1) * HEAD_DIM]
    kmean_ref[...] = jnp.mean(r[:, D_MODEL:2 * D_MODEL], axis=0, keepdims=True)


def _qkv_prompt(x, w_qkv, bsz, seq):
    nb = seq // MOBA_BLOCK
    kv_shape = jax.ShapeDtypeStruct((bsz, N_HEADS, seq, HEAD_DIM), F32)
    kv_spec = pl.BlockSpec((None, N_HEADS, MOBA_BLOCK, HEAD_DIM), lambda i: (i // nb, 0, i % nb, 0))
    return pl.pallas_call(
        _qkv_prompt_kernel,
        grid=(bsz * nb,),
        in_specs=[pl.BlockSpec((MOBA_BLOCK, D_MODEL), lambda i: (i, 0)), _resident(w_qkv.shape)],
        out_specs=[pl.BlockSpec((MOBA_BLOCK, 3 * D_MODEL), lambda i: (i, 0)), kv_spec, kv_spec,
                   pl.BlockSpec((None, 1, D_MODEL), lambda i: (i, 0, 0))],
        out_shape=[jax.ShapeDtypeStruct((bsz * seq, 3 * D_MODEL), BF16), kv_shape, kv_shape,
                   jax.ShapeDtypeStruct((bsz * nb, 1, D_MODEL), F32)],
        compiler_params=_params(("parallel",), VMEM_LIMIT),
        name="qkv_prompt",
    )(x, w_qkv)


def _qkv_sample_kernel(x_ref, w_ref, o_ref):
    o_ref[...] = jnp.dot(x_ref[...].astype(BF16), w_ref[...], preferred_element_type=F32)


def _qkv_sample(x, w_qkv):
    rows = x.shape[0]
    return pl.pallas_call(
        _qkv_sample_kernel,
        grid=(1,),
        in_specs=[_resident(x.shape), _resident(w_qkv.shape)],
        out_specs=pl.BlockSpec((rows, 3 * D_MODEL), lambda i: (0, 0)),
        out_shape=jax.ShapeDtypeStruct((rows, 3 * D_MODEL), F32),
        compiler_params=_params(("arbitrary",), VMEM_LIMIT),
        name="qkv_sample",
    )(x, w_qkv)


PAIRS_PER_ROUND = 4
STREAM_SLOTS = 3


def _first_argmax(g, idx, size, axis):
    m = jnp.max(g, axis=axis, keepdims=True)
    return jnp.min(jnp.where(g == m, idx, size), axis=axis, keepdims=True)


def _moba_prompt_kernel(it_ref, jt_ref, pt_ref, q_ref, k_ref, v_ref, km_ref, kpool, o_ref, ks_ref,
                        vt_sc, sel_sc, m_sc, l_sc, acc_sc, sa_sc, sb_sc, pbuf, psem,
                        *, n_blocks, n_own_rounds, n_rounds, n_steps):
    nb = n_blocks
    km_hi, km_lo = _split_bf16(km_ref[...])

    step = pl.program_id(0) * N_HEADS + pl.program_id(1)
    n_past_trips = (n_rounds - n_own_rounds) // 2
    chunks_per_step = 1 + n_past_trips
    n_slots, pages_per_chunk = pbuf.shape[:2]
    blocks_per_chunk = pages_per_chunk // PAGES_PER_BLOCK
    last_chunk = n_steps * chunks_per_step - 1

    def chunk_copies(chunk, slot):
        seq = chunk // chunks_per_step
        first = (chunk - seq * chunks_per_step) * pages_per_chunk
        return [pltpu.make_async_copy(kpool.at[pt_ref[seq, first + m]], pbuf.at[slot, m], psem.at[slot])
                for m in range(pages_per_chunk)]

    def stream_chunk(c):
        chunk = step * chunks_per_step + c
        slot = lax.rem(chunk, n_slots)
        for cp in chunk_copies(chunk, slot):
            cp.wait()
        ahead = chunk + n_slots - 1
        for cp in chunk_copies(jnp.minimum(ahead, last_chunk), lax.rem(ahead, n_slots)):
            cp.start()
        for n in range(blocks_per_chunk):
            tot = jnp.sum(pbuf[slot, PAGES_PER_BLOCK * n], axis=1)
            for m in range(1, PAGES_PER_BLOCK):
                tot = tot + jnp.sum(pbuf[slot, PAGES_PER_BLOCK * n + m], axis=1)
            ks_ref[:, pl.ds(c * blocks_per_chunk + n, 1), :] = tot[:, None, :]

    @pl.when(step == 0)
    def _():
        for g in range(n_slots - 1):
            for cp in chunk_copies(g, g):
                cp.start()

    ks_ref[...] = jnp.zeros_like(ks_ref)

    def rows_of(block):
        return pl.ds(pl.multiple_of(block * MOBA_BLOCK, MOBA_BLOCK), MOBA_BLOCK)

    def transpose_v(j, carry):
        vt_sc[j] = v_ref[rows_of(j), :].astype(F32).T.astype(BF16)
        return carry
    lax.fori_loop(0, nb, transpose_v, 0, unroll=4)

    blk_id = lax.broadcasted_iota(jnp.int32, (nb, MOBA_BLOCK), 0)
    key_id = lax.broadcasted_iota(jnp.int32, (MOBA_BLOCK, MOBA_BLOCK), 0)
    qry_id = lax.broadcasted_iota(jnp.int32, (MOBA_BLOCK, MOBA_BLOCK), 1)

    def select_blocks(r, carry):
        tiles = [r * PAIRS_PER_ROUND + u for u in range(PAIRS_PER_ROUND)]
        gates = []
        for i in tiles:
            q = q_ref[rows_of(i), :]
            gates.append(lax.dot_general(km_hi, q, _NT_DIMS, preferred_element_type=F32)
                         + lax.dot_general(km_lo, q, _NT_DIMS, preferred_element_type=F32))
        for i, gate in zip(tiles, gates):
            past = blk_id < i
            g = jnp.where(past, gate, NEG_INF)
            sel = jnp.zeros((nb, MOBA_BLOCK), F32)
            for _ in range(MOBA_TOPK):
                pick = blk_id == _first_argmax(g, blk_id, nb, 0)
                sel = jnp.where(pick, 1.0, sel)
                g = jnp.where(pick, NEG_INF, g)
            sel = jnp.where(past, sel, 0.0)
            for j in range(nb):
                sel_sc[i, j] = sel[j:j + 1, :]
        return carry
    lax.fori_loop(0, n_own_rounds, select_blocks, 0)

    def pair_ids(c):
        return [(it_ref[c * PAIRS_PER_ROUND + u], jt_ref[c * PAIRS_PER_ROUND + u])
                for u in range(PAIRS_PER_ROUND)]

    def score_round(c, dst):
        for u, (i, j) in enumerate(pair_ids(c)):
            dst[u] = lax.dot_general(k_ref[rows_of(j), :], q_ref[rows_of(i), :], _NT_DIMS,
                                     preferred_element_type=F32)

    def softmax_own(c, src):
        ids = pair_ids(c)
        soft = []
        for u in range(PAIRS_PER_ROUND):
            s = jnp.where(key_id <= qry_id, src[u], NEG_INF)
            m = jnp.max(s, axis=0, keepdims=True)
            p = jnp.exp2((s - m) * EXP2_SCALE)
            soft.append((m, jnp.sum(p, axis=0, keepdims=True), p.astype(BF16)))
        pv = [jnp.dot(vt_sc[j], p, preferred_element_type=F32) for (_, j), (_, _, p) in zip(ids, soft)]
        for (i, _), (m, l, _), o in zip(ids, soft, pv):
            m_sc[i] = m
            l_sc[i] = l
            acc_sc[i] = o

    def softmax_past(c, src):
        ids = pair_ids(c)
        old = [(m_sc[i], l_sc[i], acc_sc[i]) for i, _ in ids]
        soft = []
        for u, ((i, j), (m_old, l_old, _)) in enumerate(zip(ids, old)):
            s = jnp.where(sel_sc[i, j] > 0.0, src[u], NEG_INF)
            m_new = jnp.maximum(m_old, jnp.max(s, axis=0, keepdims=True))
            a = jnp.exp2((m_old - m_new) * EXP2_SCALE)
            p = jnp.exp2((s - m_new) * EXP2_SCALE)
            soft.append((m_new, a, a * l_old + jnp.sum(p, axis=0, keepdims=True), p.astype(BF16)))
        pv = [jnp.dot(vt_sc[j], p, preferred_element_type=F32) for (_, j), (_, _, _, p) in zip(ids, soft)]
        for (i, _), (m_new, a, l_new, _), (_, _, acc_old), o in zip(ids, soft, old, pv):
            m_sc[i] = m_new
            l_sc[i] = l_new
            acc_sc[i] = a * acc_old + o

    def two_rounds(softmax_round, stream):
        def body(h, carry):
            c = 2 * h
            score_round(c + 1, sb_sc)
            softmax_round(c, sa_sc)
            score_round(jnp.minimum(c + 2, n_rounds - 1), sa_sc)
            softmax_round(c + 1, sb_sc)
            if stream:
                stream_chunk(h - n_own_rounds // 2 + 1)
            return carry
        return body

    score_round(0, sa_sc)
    lax.fori_loop(0, n_own_rounds // 2, two_rounds(softmax_own, False), 0)
    stream_chunk(0)
    lax.fori_loop(n_own_rounds // 2, n_rounds // 2, two_rounds(softmax_past, True), 0)

    @pl.when(step == n_steps - 1)
    def _():
        for g in range(1, n_slots):
            for cp in chunk_copies(last_chunk, (last_chunk + g) % n_slots):
                cp.wait()

    def finish(i, carry):
        o_ref[rows_of(i), :] = (acc_sc[i] / l_sc[i]).T.astype(o_ref.dtype)
        return carry
    lax.fori_loop(0, nb, finish, 0, unroll=4)


def _pair_rounds(nb):
    assert nb % (2 * PAIRS_PER_ROUND) == 0
    rounds = [[(i, i) for i in range(r, r + PAIRS_PER_ROUND)] for r in range(0, nb, PAIRS_PER_ROUND)]
    n_own = len(rounds)
    remaining = {i: list(range(i)) for i in range(1, nb)}
    while any(remaining.values()):
        tiles = sorted((i for i in remaining if remaining[i]), key=lambda i: -len(remaining[i]))
        rounds.append([(i, remaining[i].pop()) for i in tiles[:PAIRS_PER_ROUND]])
    assert all(len(r) == PAIRS_PER_ROUND for r in rounds) and len(rounds) % 2 == 0
    return rounds, n_own


def _moba_prompt(qkv, kmean, page_table, k_pool, bsz, seq):
    nb = seq // MOBA_BLOCK
    rounds, n_own = _pair_rounds(nb)
    i_tab = jnp.array([i for r in rounds for i, _ in r], jnp.int32)
    j_tab = jnp.array([j for r in rounds for _, j in r], jnp.int32)
    bd, n_pages = page_table.shape
    n_steps = bsz * N_HEADS
    chunks_per_step = 1 + (len(rounds) - n_own) // 2
    assert bd == n_steps and n_pages % (chunks_per_step * PAGES_PER_BLOCK) == 0
    assert n_steps * chunks_per_step >= STREAM_SLOTS
    pages_per_chunk = n_pages // chunks_per_step
    assert n_pages // PAGES_PER_BLOCK <= LANES

    def head_spec(off):
        return pl.BlockSpec((seq, HEAD_DIM), lambda b, h, *_: (b, off + h))

    return pl.pallas_call(
        functools.partial(_moba_prompt_kernel, n_blocks=nb, n_own_rounds=n_own, n_rounds=len(rounds),
                          n_steps=n_steps),
        grid_spec=pltpu.PrefetchScalarGridSpec(
            num_scalar_prefetch=3,
            grid=(bsz, N_HEADS),
            in_specs=[head_spec(0), head_spec(N_HEADS), head_spec(2 * N_HEADS),
                      pl.BlockSpec((None, nb, HEAD_DIM), lambda b, h, *_: (b, 0, h)),
                      pl.BlockSpec(memory_space=pl.ANY)],
            out_specs=[head_spec(0),
                       pl.BlockSpec((None, N_HEADS, LANES, HEAD_DIM),
                                    lambda b, h, *_: (b * N_HEADS + h, 0, 0, 0))],
            scratch_shapes=[pltpu.VMEM((nb, HEAD_DIM, MOBA_BLOCK), BF16),
                            pltpu.VMEM((nb, nb, 1, MOBA_BLOCK), F32),
                            pltpu.VMEM((nb, 1, MOBA_BLOCK), F32),
                            pltpu.VMEM((nb, 1, MOBA_BLOCK), F32),
                            pltpu.VMEM((nb, HEAD_DIM, MOBA_BLOCK), F32),
                            pltpu.VMEM((PAIRS_PER_ROUND, MOBA_BLOCK, MOBA_BLOCK), F32),
                            pltpu.VMEM((PAIRS_PER_ROUND, MOBA_BLOCK, MOBA_BLOCK), F32),
                            pltpu.VMEM((STREAM_SLOTS, pages_per_chunk, N_HEADS, PAGE_SIZE, HEAD_DIM), F32),
                            pltpu.SemaphoreType.DMA((STREAM_SLOTS,))],
        ),
        out_shape=[jax.ShapeDtypeStruct((bsz * seq, D_MODEL), BF16),
                   jax.ShapeDtypeStruct((bd, N_HEADS, LANES, HEAD_DIM), F32)],
        compiler_params=_params(("arbitrary", "arbitrary"), VMEM_LIMIT),
        name="moba_prompt",
    )(i_tab, j_tab, page_table, qkv, qkv, qkv, kmean, k_pool)


def _sample_select_kernel(q_ref, ks_ref, sel_ref, *, n_full):
    t_new = q_ref.shape[0]
    inv = 1.0 / MOBA_BLOCK
    gates = []
    for h in range(N_HEADS):
        qh = jnp.concatenate([q_ref[:, h * HEAD_DIM:(h + 1) * HEAD_DIM],
                              jnp.zeros((8 - t_new, HEAD_DIM), F32)], axis=0)
        q_hi, q_lo = _split_bf16(qh)
        k_hi, k_lo = _split_bf16(ks_ref[h] * inv)
        gates.append(lax.dot_general(q_hi, k_hi, _NT_DIMS, preferred_element_type=F32)
                     + lax.dot_general(q_hi, k_lo, _NT_DIMS, preferred_element_type=F32)
                     + lax.dot_general(q_lo, k_hi, _NT_DIMS, preferred_element_type=F32))
    g = jnp.concatenate(gates, axis=0)
    col = lax.broadcasted_iota(jnp.int32, g.shape, 1)
    g = jnp.where(col < n_full, g, NEG_INF)
    out = jnp.zeros(g.shape, jnp.int32)
    for r in range(MOBA_TOPK):
        idx = _first_argmax(g, col, LANES, 1)
        out = jnp.where(col == r, idx, out)
        g = jnp.where(col == idx, NEG_INF, g)
    sel_ref[...] = out.reshape(sel_ref.shape)


def _sample_select(qkv_s, ksum, n_full):
    bd, t_new, _ = qkv_s.shape
    return pl.pallas_call(
        functools.partial(_sample_select_kernel, n_full=n_full),
        grid=(bd,),
        in_specs=[pl.BlockSpec((None, t_new, D_MODEL), lambda b: (b, 0, 0)),
                  pl.BlockSpec((None, N_HEADS, LANES, HEAD_DIM), lambda b: (b, 0, 0, 0))],
        out_specs=pl.BlockSpec((None, N_HEADS, 8, LANES), lambda b: (b, 0, 0, 0)),
        out_shape=jax.ShapeDtypeStruct((bd, N_HEADS, 8, LANES), jnp.int32),
        compiler_params=_params(("parallel",)),
        name="sample_select",
    )(qkv_s, ksum)


N_GATHER = MOBA_TOPK * PAGES_PER_BLOCK


def _sample_attn_kernel(phys_ref, q_ref, kn_ref, vn_ref, kpool, vpool, o_ref, kbuf, vbuf, sem,
                        *, t_new, n_steps):
    n_in = t_new * N_GATHER
    n_slots = kbuf.shape[0]
    ahead = n_slots - 1
    step = pl.program_id(0) * N_HEADS + pl.program_id(1)
    slot = lax.rem(step, n_slots)

    def page_copies(st, sl):
        head = lax.rem(st, N_HEADS)
        out = []
        for m in range(n_in):
            page = phys_ref[st * n_in + m]
            out.append(pltpu.make_async_copy(kpool.at[page, head], kbuf.at[sl, m], sem.at[sl]))
            out.append(pltpu.make_async_copy(vpool.at[page, head], vbuf.at[sl, m], sem.at[sl]))
        return out

    @pl.when(step == 0)
    def _():
        for st in range(min(ahead, n_steps)):
            for c in page_copies(st, st):
                c.start()

    @pl.when(step + ahead < n_steps)
    def _():
        for c in page_copies(step + ahead, lax.rem(step + ahead, n_slots)):
            c.start()

    for c in page_copies(step, slot):
        c.wait()

    row = lax.broadcasted_iota(jnp.int32, (t_new, 1), 0)
    for t in range(t_new):
        q = q_ref[t:t + 1, :]
        pages = range(t * N_GATHER, (t + 1) * N_GATHER)
        s_own = jnp.sum(kn_ref[...] * q, axis=1, keepdims=True) * ATTN_SCALE
        s_own = jnp.where(row <= t, s_own, NEG_INF)
        s_past = [jnp.sum(kbuf[slot, m] * q, axis=1, keepdims=True) * ATTN_SCALE for m in pages]
        mx = jnp.max(s_own, axis=0, keepdims=True)
        for sp in s_past:
            mx = jnp.maximum(mx, jnp.max(sp, axis=0, keepdims=True))
        p_own = jnp.exp(s_own - mx)
        l = jnp.sum(p_own, axis=0, keepdims=True)
        o = jnp.sum(p_own * vn_ref[...], axis=0, keepdims=True)
        for sp, m in zip(s_past, pages):
            pp = jnp.exp(sp - mx)
            l = l + jnp.sum(pp, axis=0, keepdims=True)
            o = o + jnp.sum(pp * vbuf[slot, m], axis=0, keepdims=True)
        o_ref[t:t + 1, :] = o / l


def _sample_attn(phys, qkv_s, k_pool, v_pool):
    bd, t_new, _ = qkv_s.shape
    n_in = t_new * N_GATHER

    def head_spec(off):
        return pl.BlockSpec((None, t_new, HEAD_DIM), lambda b, h, ph: (b, 0, off + h))

    n_slots = 3
    page_buf = pltpu.VMEM((n_slots, n_in, PAGE_SIZE, HEAD_DIM), F32)
    return pl.pallas_call(
        functools.partial(_sample_attn_kernel, t_new=t_new, n_steps=bd * N_HEADS),
        grid_spec=pltpu.PrefetchScalarGridSpec(
            num_scalar_prefetch=1,
            grid=(bd, N_HEADS),
            in_specs=[head_spec(0), head_spec(N_HEADS), head_spec(2 * N_HEADS),
                      pl.BlockSpec(memory_space=pl.ANY), pl.BlockSpec(memory_space=pl.ANY)],
            out_specs=head_spec(0),
            scratch_shapes=[page_buf, page_buf, pltpu.SemaphoreType.DMA((n_slots,))],
        ),
        out_shape=jax.ShapeDtypeStruct((bd, t_new, D_MODEL), F32),
        compiler_params=_params(("arbitrary", "arbitrary"), VMEM_LIMIT),
        name="sample_attn",
    )(phys, qkv_s, qkv_s, qkv_s, k_pool, v_pool)


def _attn_out_kernel(o_ref, x_ref, wo_ref, g_ref, b_ref, wr_ref, x3_ref, ti_ref, tw_ref):
    m = jnp.dot(o_ref[...].astype(BF16), wo_ref[...], preferred_element_type=F32)
    x3 = _ln(ALPHA * x_ref[...] + m, g_ref[...], b_ref[...])
    x3_ref[...] = x3
    logits = jnp.dot(x3.astype(BF16), wr_ref[...], preferred_element_type=F32)
    col = lax.broadcasted_iota(jnp.int32, logits.shape, 1)
    lg = jnp.where(col < N_EXPERTS, logits, NEG_INF)
    v1 = jnp.max(lg, axis=1, keepdims=True)
    i1 = jnp.min(jnp.where(lg == v1, col, LANES), axis=1, keepdims=True)
    lg2 = jnp.where(col == i1, NEG_INF, lg)
    v2 = jnp.max(lg2, axis=1, keepdims=True)
    i2 = jnp.min(jnp.where(lg2 == v2, col, LANES), axis=1, keepdims=True)
    e = jnp.exp(v2 - v1)
    w1 = 1.0 / (1.0 + e)
    w2 = e / (1.0 + e)
    ti_ref[...] = jnp.where(col == 0, i1, jnp.where(col == 1, i2, 0))
    tw_ref[...] = jnp.where(col == 0, w1, jnp.where(col == 1, w2, 0.0))


def _attn_out(o, x, w_o, g, b, w_r, *, tile):
    rows = x.shape[0]
    row_spec = pl.BlockSpec((tile, D_MODEL), lambda i: (i, 0))
    lane_spec = pl.BlockSpec((tile, LANES), lambda i: (i, 0))
    return pl.pallas_call(
        _attn_out_kernel,
        grid=(rows // tile,),
        in_specs=[row_spec, row_spec, _resident(w_o.shape), _resident(g.shape), _resident(b.shape),
                  _resident(w_r.shape)],
        out_specs=[row_spec, lane_spec, lane_spec],
        out_shape=[jax.ShapeDtypeStruct((rows, D_MODEL), F32),
                   jax.ShapeDtypeStruct((rows, LANES), jnp.int32),
                   jax.ShapeDtypeStruct((rows, LANES), F32)],
        compiler_params=_params(("parallel",), VMEM_LIMIT),
        name="attn_out_router",
    )(o, x, w_o, g, b, w_r)


def _dispatch_kernel(pos_ref, x_ref, xs_in, xs_out, sem):
    del xs_in
    n = ROW_TILE
    tiles = x_ref.shape[0] // n

    def issue(r, carry):
        for t in range(tiles):
            for k in range(TOP_K_EXPERTS):
                slot_row = pos_ref[(t * TOP_K_EXPERTS + k) * n + r]
                pltpu.make_async_copy(x_ref.at[pl.ds(t * n + r, 1)], xs_out.at[pl.ds(slot_row, 1)],
                                      sem).start(priority=k % 2)
        return carry
    lax.fori_loop(0, n, issue, 0, unroll=8)
    for _ in range(TOP_K_EXPERTS):
        pltpu.make_async_copy(x_ref, xs_out.at[pl.ds(0, tiles * n)], sem).wait()


def _dispatch(pos, x, xs, *, tiles_per_step):
    rows = x.shape[0]
    block_rows = tiles_per_step * ROW_TILE
    return pl.pallas_call(
        _dispatch_kernel,
        grid=(rows // block_rows,),
        in_specs=[pl.BlockSpec((TOP_K_EXPERTS * block_rows,), lambda i: (i,), memory_space=pltpu.SMEM),
                  pl.BlockSpec((block_rows, D_MODEL), lambda i: (i, 0)),
                  pl.BlockSpec(memory_space=pl.ANY)],
        out_specs=pl.BlockSpec(memory_space=pl.ANY),
        out_shape=jax.ShapeDtypeStruct(xs.shape, xs.dtype),
        scratch_shapes=[pltpu.SemaphoreType.DMA(())],
        input_output_aliases={2: 0},
        compiler_params=_params(("arbitrary",)),
        name="moe_dispatch",
    )(pos, x, xs)


def _moe_kernel(te_ref, tv_ref, xs_ref, wg_ref, wu_ref, wd_ref, y_ref):
    i = pl.program_id(0)

    @pl.when(tv_ref[i] == 1)
    def _():
        y_ref[...] = _swiglu_body(xs_ref[...].astype(BF16), wg_ref[...], wu_ref[...], wd_ref[...])

    @pl.when(tv_ref[i] == 0)
    def _():
        y_ref[...] = jnp.zeros_like(y_ref)


def _moe(tile_expert, tile_valid, xs, wg, wu, wd):
    n_tiles = tile_expert.shape[0]
    d_ff = wg.shape[2]
    tile_spec = pl.BlockSpec((MOE_TILE, D_MODEL), lambda i, te, tv: (i, 0))
    return pl.pallas_call(
        _moe_kernel,
        grid_spec=pltpu.PrefetchScalarGridSpec(
            num_scalar_prefetch=2,
            grid=(n_tiles,),
            in_specs=[tile_spec,
                      pl.BlockSpec((None, D_MODEL, d_ff), lambda i, te, tv: (te[i], 0, 0)),
                      pl.BlockSpec((None, D_MODEL, d_ff), lambda i, te, tv: (te[i], 0, 0)),
                      pl.BlockSpec((None, d_ff, D_MODEL), lambda i, te, tv: (te[i], 0, 0))],
            out_specs=tile_spec,
        ),
        out_shape=jax.ShapeDtypeStruct((n_tiles * MOE_TILE, D_MODEL), F32),
        compiler_params=_params(("arbitrary",), VMEM_LIMIT),
        name="moe_experts",
    )(tile_expert, tile_valid, xs, wg, wu, wd)


def _combine_kernel(pos_ref, pos_next_ref, y_hbm, x_ref, tw_ref, g_ref, b_ref, o_ref, ybuf, sem, *, n_steps):
    i = pl.program_id(0)
    slot = lax.rem(i, 2)
    n = x_ref.shape[0]
    n_rows = TOP_K_EXPERTS * n

    def gather(idx_ref, sl):
        for r in range(n_rows):
            pltpu.make_async_copy(y_hbm.at[pl.ds(idx_ref[r], 1)], ybuf.at[sl, pl.ds(r, 1)],
                                  sem.at[sl]).start(priority=r % 2)

    def wait_rows(sl):
        pltpu.make_async_copy(y_hbm.at[pl.ds(0, n_rows)], ybuf.at[sl], sem.at[sl]).wait()

    @pl.when(i == 0)
    def _():
        gather(pos_ref, slot)

    wait_rows(slot)
    gather(pos_next_ref, 1 - slot)
    tw = tw_ref[...]
    moe = tw[:, 0:1] * ybuf[slot, :n, :] + tw[:, 1:2] * ybuf[slot, n:, :]
    o_ref[...] = _ln(ALPHA * x_ref[...] + moe, g_ref[...], b_ref[...])

    @pl.when(i == n_steps - 1)
    def _():
        wait_rows(1 - slot)


def _combine(pos, y_sorted, x3, tw, g, b):
    rows = x3.shape[0]
    n_steps = rows // ROW_TILE
    row_spec = pl.BlockSpec((ROW_TILE, D_MODEL), lambda i: (i, 0))
    idx_block = (TOP_K_EXPERTS * ROW_TILE,)
    return pl.pallas_call(
        functools.partial(_combine_kernel, n_steps=n_steps),
        grid=(n_steps,),
        in_specs=[pl.BlockSpec(idx_block, lambda i: (i,), memory_space=pltpu.SMEM),
                  pl.BlockSpec(idx_block, lambda i: (jnp.minimum(i + 1, n_steps - 1),), memory_space=pltpu.SMEM),
                  pl.BlockSpec(memory_space=pl.ANY), row_spec,
                  pl.BlockSpec((ROW_TILE, LANES), lambda i: (i, 0)),
                  _resident(g.shape), _resident(b.shape)],
        out_specs=row_spec,
        out_shape=jax.ShapeDtypeStruct((rows, D_MODEL), F32),
        scratch_shapes=[pltpu.VMEM((2, TOP_K_EXPERTS * ROW_TILE, D_MODEL), F32), pltpu.SemaphoreType.DMA((2,))],
        compiler_params=_params(("arbitrary",), VMEM_LIMIT),
        name="moe_combine",
    )(pos, pos, y_sorted, x3, tw, g, b)


def _route(top_i, n_tiles):
    rows = top_i.shape[0]
    e_flat = top_i.reshape(-1)
    experts = jnp.arange(N_EXPERTS, dtype=jnp.int32)
    onehot = (e_flat[:, None] == experts[None, :]).astype(jnp.int32)
    csum = jnp.cumsum(onehot, axis=0)
    rank = jnp.sum(onehot * csum, axis=1) - 1
    tiles_e = (csum[-1] + MOE_TILE - 1) // MOE_TILE
    tile_end = jnp.cumsum(tiles_e)
    tile_start = tile_end - tiles_e
    pos = jnp.sum(onehot * tile_start[None, :], axis=1) * MOE_TILE + rank
    tile_id = jnp.arange(n_tiles, dtype=jnp.int32)
    tile_expert = jnp.sum((tile_id[:, None] >= tile_end[None, :]).astype(jnp.int32), axis=1)
    tile_expert = jnp.minimum(tile_expert, N_EXPERTS - 1)
    tile_valid = (tile_id < tile_end[-1]).astype(jnp.int32)
    pos_tiles = pos.reshape(rows // ROW_TILE, ROW_TILE, TOP_K_EXPERTS).transpose(0, 2, 1).reshape(-1)
    return tile_expert, tile_valid, pos_tiles.astype(jnp.int32)


def kernel(x_prompt, x_sample, cache_k, cache_v, page_table, a_w_in, a_ln_g, a_ln_b, a_w_s, a_b_s, a_w_out,
           b_w_qkv, b_w_o, ffn_w_gate, ffn_w_up, ffn_w_down, moe_w_router, moe_w_gate, moe_w_up, moe_w_down,
           ln1_g, ln1_b, ln2_g, ln2_b):
    bsz, seq, _ = x_prompt.shape
    bd, t_new, _ = x_sample.shape
    rows_p, rows_s = bsz * seq, bd * t_new
    assert rows_s == CHUNK and CHUNK % t_new == 0 and seq % MOBA_BLOCK == 0
    n_pages = page_table.shape[1]
    past = n_pages * PAGE_SIZE
    assert past % MOBA_BLOCK == 0
    n_full = past // MOBA_BLOCK
    assert MOBA_TOPK <= n_full <= LANES

    xp = x_prompt.reshape(rows_p, D_MODEL)
    xs = x_sample.reshape(rows_s, D_MODEL)
    row = lambda a: a.reshape(1, -1)

    w_in = a_w_in[0].astype(BF16)
    w_out = a_w_out[0].astype(BF16)
    tril = jnp.tril(jnp.ones((CHUNK, CHUNK), bool))
    wmix_p = jnp.where(tril, a_w_s[0], 0.0).astype(BF16)
    bmix_p = jnp.repeat(a_b_s[0].T, LANES, axis=1)
    w_small = jnp.where(tril[:t_new, :t_new], a_w_s[0][:, :t_new, :t_new], 0.0)
    wmix_s = jnp.einsum("ab,gij->gaibj", jnp.eye(bd, dtype=F32), w_small).reshape(
        A_GROUPS, rows_s, rows_s).astype(BF16)
    bmix_s = jnp.repeat(jnp.tile(a_b_s[0][:, :t_new], (1, bd)).T, LANES, axis=1)
    gm_args = (w_in, row(a_ln_g[0]), row(a_ln_b[0]))
    gm_tail = (w_out, row(ln1_g[0]), row(ln1_b[0]))
    (xp,) = _gmlp(xp, *gm_args, wmix_p, bmix_p, *gm_tail, tile=512, emit_v=False)
    xs, v_rows = _gmlp(xs, *gm_args, wmix_s, bmix_s, *gm_tail, tile=rows_s, emit_v=True)

    ffn = (ffn_w_gate[0].astype(BF16), ffn_w_up[0].astype(BF16), ffn_w_down[0].astype(BF16),
           row(ln2_g[0]), row(ln2_b[0]))
    xp = _swiglu(xp, *ffn, tile=512)
    xs = _swiglu(xs, *ffn, tile=rows_s)

    w_qkv = b_w_qkv[0].astype(BF16)
    qkv_p, k_p, v_p, kmean = _qkv_prompt(xp, w_qkv, bsz, seq)
    o_p, ksum = _moba_prompt(qkv_p, kmean.reshape(bsz, seq // MOBA_BLOCK, D_MODEL), page_table, cache_k[0],
                             bsz, seq)

    qkv_s = _qkv_sample(xs, w_qkv).reshape(bd, t_new, 3 * D_MODEL)
    sel = _sample_select(qkv_s, ksum, n_full)[:, :, :t_new, :MOBA_TOPK]
    logical = (sel[..., None] * PAGES_PER_BLOCK
               + jnp.arange(PAGES_PER_BLOCK, dtype=jnp.int32)).reshape(bd, -1)
    phys = jnp.take_along_axis(page_table, logical, axis=1).reshape(-1)
    o_s = _sample_attn(phys, qkv_s, cache_k[0], cache_v[0]).reshape(rows_s, D_MODEL)
    kv_s = qkv_s.reshape(bd, t_new, 3, N_HEADS, HEAD_DIM)
    k_s = jnp.transpose(kv_s[:, :, 1], (0, 2, 1, 3))
    v_s = jnp.transpose(kv_s[:, :, 2], (0, 2, 1, 3))

    w_o = b_w_o[0].astype(BF16)
    w_r = jnp.pad(moe_w_router[0], ((0, 0), (0, LANES - N_EXPERTS))).astype(BF16)
    tail = (w_o, row(ln1_g[1]), row(ln1_b[1]), w_r)
    x3_p, ti_p, tw_p = _attn_out(o_p, xp, *tail, tile=512)
    x3_s, ti_s, tw_s = _attn_out(o_s, xs, *tail, tile=rows_s)

    top_i = jnp.concatenate([ti_p[:, :TOP_K_EXPERTS], ti_s[:, :TOP_K_EXPERTS]], axis=0)
    n_slots_used = TOP_K_EXPERTS * (rows_p + rows_s)
    n_tiles = n_slots_used // MOE_TILE + N_EXPERTS
    tile_expert, tile_valid, pos = _route(top_i, n_tiles)
    pos_p, pos_s = pos[:TOP_K_EXPERTS * rows_p], pos[TOP_K_EXPERTS * rows_p:]
    slots = jnp.zeros((n_tiles * MOE_TILE, D_MODEL), F32)
    slots = _dispatch(pos_p, x3_p, slots, tiles_per_step=2)
    slots = _dispatch(pos_s, x3_s, slots, tiles_per_step=1)
    y_slots = _moe(tile_expert, tile_valid, slots, moe_w_gate[0].astype(BF16),
                   moe_w_up[0].astype(BF16), moe_w_down[0].astype(BF16))
    ln2 = (row(ln2_g[1]), row(ln2_b[1]))
    y_prompt = _combine(pos_p, y_slots, x3_p, tw_p, *ln2).reshape(bsz, seq, D_MODEL)
    y_sample = _combine(pos_s, y_slots, x3_s, tw_s, *ln2).reshape(bd, t_new, D_MODEL)

    return (y_prompt, y_sample, v_rows.reshape(1, bd, t_new, D_MODEL), k_p[None], v_p[None],
            k_s[None], v_s[None])
```

```python
import functools

import jax
import jax.numpy as jnp
from jax import lax
from jax.experimental import pallas as pl
from jax.experimental.pallas import tpu as pltpu

F32 = jnp.float32
BF16 = jnp.bfloat16

D_MODEL = 1024
DEPTH = 2
CHUNK = 128
A_GROUPS = 8
N_HEADS = 8
HEAD_DIM = 128
MOBA_BLOCK = 256
MOBA_TOPK = 3
PAGE_SIZE = 128
PAGES_PER_BLOCK = MOBA_BLOCK // PAGE_SIZE
N_EXPERTS = 8
TOP_K_EXPERTS = 2
ATTN_SCALE = HEAD_DIM ** -0.5
EXP2_SCALE = ATTN_SCALE * 1.4426950408889634
ALPHA = (2 * DEPTH) ** 0.25
LN_EPS = 1e-5
INV_SQRT2 = 0.7071067811865476
NEG_INF = float("-inf")

LANES = 128
MOE_TILE = 256
ROW_TILE = 128
VMEM_LIMIT = 56 * 1024 * 1024

_NT_DIMS = (((1,), (1,)), ((), ()))


def _ln(x, g, b):
    mu = jnp.mean(x, axis=-1, keepdims=True)
    xc = x - mu
    var = jnp.mean(xc * xc, axis=-1, keepdims=True)
    return xc * lax.rsqrt(var + LN_EPS) * g + b


def _split_bf16(a):
    hi = a.astype(BF16)
    return hi, (a - hi.astype(F32)).astype(BF16)


def _resident(shape):
    nd = len(shape)
    return pl.BlockSpec(shape, lambda *_: (0,) * nd, pipeline_mode=pl.Buffered(1))


def _params(sem, vmem=None):
    return pltpu.CompilerParams(dimension_semantics=sem, vmem_limit_bytes=vmem)


def _gmlp_kernel(x_ref, win_ref, lng_ref, lnb_ref, wmix_ref, bmix_ref, wout_ref, g1_ref, b1_ref,
                 *out_refs, n_chunks, emit_v):
    x = x_ref[...]
    h = jnp.dot(x.astype(BF16), win_ref[...], preferred_element_type=F32)
    h = 0.5 * h * (1.0 + lax.erf(h * INV_SQRT2))
    u = h[:, :D_MODEL]
    v = _ln(h[:, D_MODEL:], lng_ref[...], lnb_ref[...])
    if emit_v:
        out_refs[1][...] = v
    vb = v.astype(BF16)
    rows = []
    for c in range(n_chunks):
        cols = []
        for g in range(A_GROUPS):
            vg = vb[c * CHUNK:(c + 1) * CHUNK, g * LANES:(g + 1) * LANES]
            cols.append(jnp.dot(wmix_ref[g], vg, preferred_element_type=F32))
        rows.append(jnp.concatenate(cols, axis=1) + bmix_ref[...])
    z = rows[0] if n_chunks == 1 else jnp.concatenate(rows, axis=0)
    y = jnp.dot((u * z).astype(BF16), wout_ref[...], preferred_element_type=F32)
    out_refs[0][...] = _ln(ALPHA * x + y, g1_ref[...], b1_ref[...])


def _gmlp(x, w_in, ln_g, ln_b, wmix, bmix, w_out, g1, b1, *, tile, emit_v):
    rows = x.shape[0]
    n_chunks = tile // CHUNK
    row_spec = pl.BlockSpec((tile, D_MODEL), lambda i: (i, 0))
    out_shape = [jax.ShapeDtypeStruct((rows, D_MODEL), F32)]
    out_specs = [row_spec]
    if emit_v:
        out_shape.append(jax.ShapeDtypeStruct((rows, D_MODEL), F32))
        out_specs.append(row_spec)
    return pl.pallas_call(
        functools.partial(_gmlp_kernel, n_chunks=n_chunks, emit_v=emit_v),
        grid=(rows // tile,),
        in_specs=[row_spec, _resident(w_in.shape), _resident(ln_g.shape), _resident(ln_b.shape),
                  _resident(wmix.shape), _resident(bmix.shape), _resident(w_out.shape),
                  _resident(g1.shape), _resident(b1.shape)],
        out_specs=out_specs,
        out_shape=out_shape,
        compiler_params=_params(("parallel",), VMEM_LIMIT),
        name="gmlp_mixer",
    )(x, w_in, ln_g, ln_b, wmix, bmix, w_out, g1, b1)


def _swiglu_body(xb, wg, wu, wd):
    a = jnp.dot(xb, wg, preferred_element_type=F32)
    up = jnp.dot(xb, wu, preferred_element_type=F32)
    h = (a * jax.nn.sigmoid(a)) * up
    return jnp.dot(h.astype(BF16), wd, preferred_element_type=F32)


def _swiglu_kernel(x_ref, wg_ref, wu_ref, wd_ref, g_ref, b_ref, o_ref):
    x = x_ref[...]
    f = _swiglu_body(x.astype(BF16), wg_ref[...], wu_ref[...], wd_ref[...])
    o_ref[...] = _ln(ALPHA * x + f, g_ref[...], b_ref[...])


def _swiglu(x, wg, wu, wd, g, b, *, tile):
    rows = x.shape[0]
    row_spec = pl.BlockSpec((tile, D_MODEL), lambda i: (i, 0))
    return pl.pallas_call(
        _swiglu_kernel,
        grid=(rows // tile,),
        in_specs=[row_spec, _resident(wg.shape), _resident(wu.shape), _resident(wd.shape),
                  _resident(g.shape), _resident(b.shape)],
        out_specs=row_spec,
        out_shape=jax.ShapeDtypeStruct((rows, D_MODEL), F32),
        compiler_params=_params(("parallel",), VMEM_LIMIT),
        name="dense_swiglu",
    )(x, wg, wu, wd, g, b)


def _qkv_prompt_kernel(x_ref, w_ref, qkv_ref, k_ref, v_ref, kmean_ref):
    r = jnp.dot(x_ref[...].astype(BF16), w_ref[...], preferred_element_type=F32)
    qkv_ref[...] = r.astype(BF16)
    for h in range(N_HEADS):
        k_ref[h] = r[:, D_MODEL + h * HEAD_DIM:D_MODEL + (h + 1) * HEAD_DIM]
        v_ref[h] = r[:, 2 * D_MODEL + h * HEAD_DIM:2 * D_MODEL + (h + 1) * HEAD_DIM]
    kmean_ref[...] = jnp.mean(r[:, D_MODEL:2 * D_MODEL], axis=0, keepdims=True)


def _qkv_prompt(x, w_qkv, bsz, seq):
    nb = seq // MOBA_BLOCK
    kv_shape = jax.ShapeDtypeStruct((bsz, N_HEADS, seq, HEAD_DIM), F32)
    kv_spec = pl.BlockSpec((None, N_HEADS, MOBA_BLOCK, HEAD_DIM), lambda i: (i // nb, 0, i % nb, 0))
    return pl.pallas_call(
        _qkv_prompt_kernel,
        grid=(bsz * nb,),
        in_specs=[pl.BlockSpec((MOBA_BLOCK, D_MODEL), lambda i: (i, 0)), _resident(w_qkv.shape)],
        out_specs=[pl.BlockSpec((MOBA_BLOCK, 3 * D_MODEL), lambda i: (i, 0)), kv_spec, kv_spec,
                   pl.BlockSpec((None, 1, D_MODEL), lambda i: (i, 0, 0))],
        out_shape=[jax.ShapeDtypeStruct((bsz * seq, 3 * D_MODEL), BF16), kv_shape, kv_shape,
                   jax.ShapeDtypeStruct((bsz * nb, 1, D_MODEL), F32)],
        compiler_params=_params(("parallel",), VMEM_LIMIT),
        name="qkv_prompt",
    )(x, w_qkv)


def _qkv_sample_kernel(x_ref, w_ref, o_ref):
    o_ref[...] = jnp.dot(x_ref[...].astype(BF16), w_ref[...], preferred_element_type=F32)


def _qkv_sample(x, w_qkv):
    rows = x.shape[0]
    return pl.pallas_call(
        _qkv_sample_kernel,
        grid=(1,),
        in_specs=[_resident(x.shape), _resident(w_qkv.shape)],
        out_specs=pl.BlockSpec((rows, 3 * D_MODEL), lambda i: (0, 0)),
        out_shape=jax.ShapeDtypeStruct((rows, 3 * D_MODEL), F32),
        compiler_params=_params(("arbitrary",), VMEM_LIMIT),
        name="qkv_sample",
    )(x, w_qkv)


PAIRS_PER_ROUND = 4
STREAM_SLOTS = 4


def _first_argmax(g, idx, size, axis):
    m = jnp.max(g, axis=axis, keepdims=True)
    return jnp.min(jnp.where(g == m, idx, size), axis=axis, keepdims=True)


def _moba_prompt_kernel(it_ref, jt_ref, pt_ref, q_ref, k_ref, v_ref, km_ref, kpool, o_ref, ks_ref,
                        vt_sc, sel_sc, m_sc, l_sc, acc_sc, sa_sc, sb_sc, pbuf, psem,
                        *, n_blocks, n_own_rounds, n_rounds, n_steps):
    nb = n_blocks
    km_hi, km_lo = _split_bf16(km_ref[...])

    step = pl.program_id(0) * N_HEADS + pl.program_id(1)
    n_past_trips = (n_rounds - n_own_rounds) // 2
    chunks_per_step = 1 + n_past_trips
    n_slots, pages_per_chunk = pbuf.shape[:2]
    blocks_per_chunk = pages_per_chunk // PAGES_PER_BLOCK
    last_chunk = n_steps * chunks_per_step - 1

    def chunk_copies(chunk, slot):
        seq = chunk // chunks_per_step
        first = (chunk - seq * chunks_per_step) * pages_per_chunk
        return [pltpu.make_async_copy(kpool.at[pt_ref[seq, first + m]], pbuf.at[slot, m], psem.at[slot])
                for m in range(pages_per_chunk)]

    def stream_chunk(c):
        chunk = step * chunks_per_step + c
        slot = lax.rem(chunk, n_slots)
        for cp in chunk_copies(chunk, slot):
            cp.wait()
        ahead = chunk + n_slots - 1
        for cp in chunk_copies(jnp.minimum(ahead, last_chunk), lax.rem(ahead, n_slots)):
            cp.start()
        for n in range(blocks_per_chunk):
            tot = jnp.sum(pbuf[slot, PAGES_PER_BLOCK * n], axis=1)
            for m in range(1, PAGES_PER_BLOCK):
                tot = tot + jnp.sum(pbuf[slot, PAGES_PER_BLOCK * n + m], axis=1)
            ks_ref[:, pl.ds(c * blocks_per_chunk + n, 1), :] = tot[:, None, :]

    @pl.when(step == 0)
    def _():
        for g in range(n_slots - 1):
            for cp in chunk_copies(g, g):
                cp.start()

    ks_ref[...] = jnp.zeros_like(ks_ref)

    def rows_of(block):
        return pl.ds(pl.multiple_of(block * MOBA_BLOCK, MOBA_BLOCK), MOBA_BLOCK)

    def transpose_v(j, carry):
        vt_sc[j] = v_ref[rows_of(j), :].astype(F32).T.astype(BF16)
        return carry
    lax.fori_loop(0, nb, transpose_v, 0, unroll=4)

    blk_id = lax.broadcasted_iota(jnp.int32, (nb, MOBA_BLOCK), 0)
    key_id = lax.broadcasted_iota(jnp.int32, (MOBA_BLOCK, MOBA_BLOCK), 0)
    qry_id = lax.broadcasted_iota(jnp.int32, (MOBA_BLOCK, MOBA_BLOCK), 1)

    def select_blocks(r, carry):
        tiles = [r * PAIRS_PER_ROUND + u for u in range(PAIRS_PER_ROUND)]
        gates = []
        for i in tiles:
            q = q_ref[rows_of(i), :]
            gates.append(lax.dot_general(km_hi, q, _NT_DIMS, preferred_element_type=F32)
                         + lax.dot_general(km_lo, q, _NT_DIMS, preferred_element_type=F32))
        for i, gate in zip(tiles, gates):
            past = blk_id < i
            g = jnp.where(past, gate, NEG_INF)
            sel = jnp.zeros((nb, MOBA_BLOCK), F32)
            for _ in range(MOBA_TOPK):
                pick = blk_id == _first_argmax(g, blk_id, nb, 0)
                sel = jnp.where(pick, 1.0, sel)
                g = jnp.where(pick, NEG_INF, g)
            sel = jnp.where(past, sel, 0.0)
            for j in range(nb):
                sel_sc[i, j] = sel[j:j + 1, :]
        return carry
    lax.fori_loop(0, n_own_rounds, select_blocks, 0)

    def pair_ids(c):
        return [(it_ref[c * PAIRS_PER_ROUND + u], jt_ref[c * PAIRS_PER_ROUND + u])
                for u in range(PAIRS_PER_ROUND)]

    def score_round(c, dst):
        for u, (i, j) in enumerate(pair_ids(c)):
            dst[u] = lax.dot_general(k_ref[rows_of(j), :], q_ref[rows_of(i), :], _NT_DIMS,
                                     preferred_element_type=F32)

    def softmax_own(c, src):
        ids = pair_ids(c)
        soft = []
        for u in range(PAIRS_PER_ROUND):
            s = jnp.where(key_id <= qry_id, src[u], NEG_INF)
            m = jnp.max(s, axis=0, keepdims=True)
            p = jnp.exp2((s - m) * EXP2_SCALE)
            soft.append((m, jnp.sum(p, axis=0, keepdims=True), p.astype(BF16)))
        pv = [jnp.dot(vt_sc[j], p, preferred_element_type=F32) for (_, j), (_, _, p) in zip(ids, soft)]
        for (i, _), (m, l, _), o in zip(ids, soft, pv):
            m_sc[i] = m
            l_sc[i] = l
            acc_sc[i] = o

    def softmax_past(c, src):
        ids = pair_ids(c)
        old = [(m_sc[i], l_sc[i], acc_sc[i]) for i, _ in ids]
        soft = []
        for u, ((i, j), (m_old, l_old, _)) in enumerate(zip(ids, old)):
            s = jnp.where(sel_sc[i, j] > 0.0, src[u], NEG_INF)
            m_new = jnp.maximum(m_old, jnp.max(s, axis=0, keepdims=True))
            a = jnp.exp2((m_old - m_new) * EXP2_SCALE)
            p = jnp.exp2((s - m_new) * EXP2_SCALE)
            soft.append((m_new, a, a * l_old + jnp.sum(p, axis=0, keepdims=True), p.astype(BF16)))
        pv = [jnp.dot(vt_sc[j], p, preferred_element_type=F32) for (_, j), (_, _, _, p) in zip(ids, soft)]
        for (i, _), (m_new, a, l_new, _), (_, _, acc_old), o in zip(ids, soft, old, pv):
            m_sc[i] = m_new
            l_sc[i] = l_new
            acc_sc[i] = a * acc_old + o

    def two_rounds(softmax_round, stream):
        def body(h, carry):
            c = 2 * h
            score_round(c + 1, sb_sc)
            softmax_round(c, sa_sc)
            score_round(jnp.minimum(c + 2, n_rounds - 1), sa_sc)
            softmax_round(c + 1, sb_sc)
            if stream:
                stream_chunk(h - n_own_rounds // 2 + 1)
            return carry
        return body

    score_round(0, sa_sc)
    lax.fori_loop(0, n_own_rounds // 2, two_rounds(softmax_own, False), 0)
    stream_chunk(0)
    lax.fori_loop(n_own_rounds // 2, n_rounds // 2, two_rounds(softmax_past, True), 0)

    @pl.when(step == n_steps - 1)
    def _():
        for g in range(1, n_slots):
            for cp in chunk_copies(last_chunk, (last_chunk + g) % n_slots):
                cp.wait()

    def finish(i, carry):
        o_ref[rows_of(i), :] = (acc_sc[i] / l_sc[i]).T.astype(o_ref.dtype)
        return carry
    lax.fori_loop(0, nb, finish, 0, unroll=4)


def _pair_rounds(nb):
    assert nb % (2 * PAIRS_PER_ROUND) == 0
    rounds = [[(i, i) for i in range(r, r + PAIRS_PER_ROUND)] for r in range(0, nb, PAIRS_PER_ROUND)]
    n_own = len(rounds)
    remaining = {i: list(range(i)) for i in range(1, nb)}
    while any(remaining.values()):
        tiles = sorted((i for i in remaining if remaining[i]), key=lambda i: -len(remaining[i]))
        rounds.append([(i, remaining[i].pop()) for i in tiles[:PAIRS_PER_ROUND]])
    assert all(len(r) == PAIRS_PER_ROUND for r in rounds) and len(rounds) % 2 == 0
    return rounds, n_own


def _moba_prompt(qkv, kmean, page_table, k_pool, bsz, seq):
    nb = seq // MOBA_BLOCK
    rounds, n_own = _pair_rounds(nb)
    i_tab = jnp.array([i for r in rounds for i, _ in r], jnp.int32)
    j_tab = jnp.array([j for r in rounds for _, j in r], jnp.int32)
    bd, n_pages = page_table.shape
    n_steps = bsz * N_HEADS
    chunks_per_step = 1 + (len(rounds) - n_own) // 2
    assert bd == n_steps and n_pages % (chunks_per_step * PAGES_PER_BLOCK) == 0
    assert n_steps * chunks_per_step >= STREAM_SLOTS
    pages_per_chunk = n_pages // chunks_per_step
    assert n_pages // PAGES_PER_BLOCK <= LANES

    def head_spec(off):
        return pl.BlockSpec((seq, HEAD_DIM), lambda b, h, *_: (b, off + h))

    return pl.pallas_call(
        functools.partial(_moba_prompt_kernel, n_blocks=nb, n_own_rounds=n_own, n_rounds=len(rounds),
                          n_steps=n_steps),
        grid_spec=pltpu.PrefetchScalarGridSpec(
            num_scalar_prefetch=3,
            grid=(bsz, N_HEADS),
            in_specs=[head_spec(0), head_spec(N_HEADS), head_spec(2 * N_HEADS),
                      pl.BlockSpec((None, nb, HEAD_DIM), lambda b, h, *_: (b, 0, h)),
                      pl.BlockSpec(memory_space=pl.ANY)],
            out_specs=[head_spec(0),
                       pl.BlockSpec((None, N_HEADS, LANES, HEAD_DIM),
                                    lambda b, h, *_: (b * N_HEADS + h, 0, 0, 0))],
            scratch_shapes=[pltpu.VMEM((nb, HEAD_DIM, MOBA_BLOCK), BF16),
                            pltpu.VMEM((nb, nb, 1, MOBA_BLOCK), F32),
                            pltpu.VMEM((nb, 1, MOBA_BLOCK), F32),
                            pltpu.VMEM((nb, 1, MOBA_BLOCK), F32),
                            pltpu.VMEM((nb, HEAD_DIM, MOBA_BLOCK), F32),
                            pltpu.VMEM((PAIRS_PER_ROUND, MOBA_BLOCK, MOBA_BLOCK), F32),
                            pltpu.VMEM((PAIRS_PER_ROUND, MOBA_BLOCK, MOBA_BLOCK), F32),
                            pltpu.VMEM((STREAM_SLOTS, pages_per_chunk, N_HEADS, PAGE_SIZE, HEAD_DIM), F32),
                            pltpu.SemaphoreType.DMA((STREAM_SLOTS,))],
        ),
        out_shape=[jax.ShapeDtypeStruct((bsz * seq, D_MODEL), BF16),
                   jax.ShapeDtypeStruct((bd, N_HEADS, LANES, HEAD_DIM), F32)],
        compiler_params=_params(("arbitrary", "arbitrary"), VMEM_LIMIT),
        name="moba_prompt",
    )(i_tab, j_tab, page_table, qkv, qkv, qkv, kmean, k_pool)


def _sample_select_kernel(q_ref, ks_ref, sel_ref, *, n_full):
    t_new = q_ref.shape[0]
    inv = 1.0 / MOBA_BLOCK
    gates = []
    for h in range(N_HEADS):
        qh = jnp.concatenate([q_ref[:, h * HEAD_DIM:(h + 1) * HEAD_DIM],
                              jnp.zeros((8 - t_new, HEAD_DIM), F32)], axis=0)
        q_hi, q_lo = _split_bf16(qh)
        k_hi, k_lo = _split_bf16(ks_ref[h] * inv)
        gates.append(lax.dot_general(q_hi, k_hi, _NT_DIMS, preferred_element_type=F32)
                     + lax.dot_general(q_hi, k_lo, _NT_DIMS, preferred_element_type=F32)
                     + lax.dot_general(q_lo, k_hi, _NT_DIMS, preferred_element_type=F32))
    g = jnp.concatenate(gates, axis=0)
    col = lax.broadcasted_iota(jnp.int32, g.shape, 1)
    g = jnp.where(col < n_full, g, NEG_INF)
    out = jnp.zeros(g.shape, jnp.int32)
    for r in range(MOBA_TOPK):
        idx = _first_argmax(g, col, LANES, 1)
        out = jnp.where(col == r, idx, out)
        g = jnp.where(col == idx, NEG_INF, g)
    sel_ref[...] = out.reshape(sel_ref.shape)


def _sample_select(qkv_s, ksum, n_full):
    bd, t_new, _ = qkv_s.shape
    return pl.pallas_call(
        functools.partial(_sample_select_kernel, n_full=n_full),
        grid=(bd,),
        in_specs=[pl.BlockSpec((None, t_new, D_MODEL), lambda b: (b, 0, 0)),
                  pl.BlockSpec((None, N_HEADS, LANES, HEAD_DIM), lambda b: (b, 0, 0, 0))],
        out_specs=pl.BlockSpec((None, N_HEADS, 8, LANES), lambda b: (b, 0, 0, 0)),
        out_shape=jax.ShapeDtypeStruct((bd, N_HEADS, 8, LANES), jnp.int32),
        compiler_params=_params(("parallel",)),
        name="sample_select",
    )(qkv_s, ksum)


N_GATHER = MOBA_TOPK * PAGES_PER_BLOCK


def _sample_attn_kernel(phys_ref, q_ref, kn_ref, vn_ref, kpool, vpool, o_ref, kbuf, vbuf, sem,
                        *, t_new, n_steps):
    n_in = t_new * N_GATHER
    n_slots = kbuf.shape[0]
    ahead = n_slots - 1
    step = pl.program_id(0) * N_HEADS + pl.program_id(1)
    slot = lax.rem(step, n_slots)

    def page_copies(st, sl):
        head = lax.rem(st, N_HEADS)
        out = []
        for m in range(n_in):
            page = phys_ref[st * n_in + m]
            out.append(pltpu.make_async_copy(kpool.at[page, head], kbuf.at[sl, m], sem.at[sl]))
            out.append(pltpu.make_async_copy(vpool.at[page, head], vbuf.at[sl, m], sem.at[sl]))
        return out

    @pl.when(step == 0)
    def _():
        for st in range(min(ahead, n_steps)):
            for c in page_copies(st, st):
                c.start()

    @pl.when(step + ahead < n_steps)
    def _():
        for c in page_copies(step + ahead, lax.rem(step + ahead, n_slots)):
            c.start()

    for c in page_copies(step, slot):
        c.wait()

    row = lax.broadcasted_iota(jnp.int32, (t_new, 1), 0)
    for t in range(t_new):
        q = q_ref[t:t + 1, :]
        pages = range(t * N_GATHER, (t + 1) * N_GATHER)
        s_own = jnp.sum(kn_ref[...] * q, axis=1, keepdims=True) * ATTN_SCALE
        s_own = jnp.where(row <= t, s_own, NEG_INF)
        s_past = [jnp.sum(kbuf[slot, m] * q, axis=1, keepdims=True) * ATTN_SCALE for m in pages]
        mx = jnp.max(s_own, axis=0, keepdims=True)
        for sp in s_past:
            mx = jnp.maximum(mx, jnp.max(sp, axis=0, keepdims=True))
        p_own = jnp.exp(s_own - mx)
        l = jnp.sum(p_own, axis=0, keepdims=True)
        o = jnp.sum(p_own * vn_ref[...], axis=0, keepdims=True)
        for sp, m in zip(s_past, pages):
            pp = jnp.exp(sp - mx)
            l = l + jnp.sum(pp, axis=0, keepdims=True)
            o = o + jnp.sum(pp * vbuf[slot, m], axis=0, keepdims=True)
        o_ref[t:t + 1, :] = o / l


def _sample_attn(phys, qkv_s, k_pool, v_pool):
    bd, t_new, _ = qkv_s.shape
    n_in = t_new * N_GATHER

    def head_spec(off):
        return pl.BlockSpec((None, t_new, HEAD_DIM), lambda b, h, ph: (b, 0, off + h))

    n_slots = 3
    page_buf = pltpu.VMEM((n_slots, n_in, PAGE_SIZE, HEAD_DIM), F32)
    return pl.pallas_call(
        functools.partial(_sample_attn_kernel, t_new=t_new, n_steps=bd * N_HEADS),
        grid_spec=pltpu.PrefetchScalarGridSpec(
            num_scalar_prefetch=1,
            grid=(bd, N_HEADS),
            in_specs=[head_spec(0), head_spec(N_HEADS), head_spec(2 * N_HEADS),
                      pl.BlockSpec(memory_space=pl.ANY), pl.BlockSpec(memory_space=pl.ANY)],
            out_specs=head_spec(0),
            scratch_shapes=[page_buf, page_buf, pltpu.SemaphoreType.DMA((n_slots,))],
        ),
        out_shape=jax.ShapeDtypeStruct((bd, t_new, D_MODEL), F32),
        compiler_params=_params(("arbitrary", "arbitrary"), VMEM_LIMIT),
        name="sample_attn",
    )(phys, qkv_s, qkv_s, qkv_s, k_pool, v_pool)


def _attn_out_kernel(o_ref, x_ref, wo_ref, g_ref, b_ref, wr_ref, x3_ref, ti_ref, tw_ref):
    m = jnp.dot(o_ref[...].astype(BF16), wo_ref[...], preferred_element_type=F32)
    x3 = _ln(ALPHA * x_ref[...] + m, g_ref[...], b_ref[...])
    x3_ref[...] = x3
    logits = lax.dot_general(wr_ref[...], x3.astype(BF16), _NT_DIMS, preferred_element_type=F32)
    n_rows = logits.shape[0]
    row = lax.broadcasted_iota(jnp.int32, logits.shape, 0)
    lg = jnp.where(row < N_EXPERTS, logits, NEG_INF)
    v1 = jnp.max(lg, axis=0, keepdims=True)
    i1 = jnp.min(jnp.where(lg == v1, row, n_rows), axis=0, keepdims=True)
    lg2 = jnp.where(row == i1, NEG_INF, lg)
    v2 = jnp.max(lg2, axis=0, keepdims=True)
    i2 = jnp.min(jnp.where(lg2 == v2, row, n_rows), axis=0, keepdims=True)
    e = jnp.exp(v2 - v1)
    w1 = 1.0 / (1.0 + e)
    w2 = e / (1.0 + e)
    out_row = lax.broadcasted_iota(jnp.int32, ti_ref.shape, 0)
    ti_ref[...] = jnp.where(out_row == 0, i1, jnp.where(out_row == 1, i2, 0))
    tw_ref[...] = jnp.where(out_row == 0, w1, jnp.where(out_row == 1, w2, 0.0))


def _attn_out(o, x, w_o, g, b, w_r, *, tile):
    rows = x.shape[0]
    row_spec = pl.BlockSpec((tile, D_MODEL), lambda i: (i, 0))
    top_spec = pl.BlockSpec((8, tile), lambda i: (0, i))
    return pl.pallas_call(
        _attn_out_kernel,
        grid=(rows // tile,),
        in_specs=[row_spec, row_spec, _resident(w_o.shape), _resident(g.shape), _resident(b.shape),
                  _resident(w_r.shape)],
        out_specs=[row_spec, top_spec, top_spec],
        out_shape=[jax.ShapeDtypeStruct((rows, D_MODEL), F32),
                   jax.ShapeDtypeStruct((8, rows), jnp.int32),
                   jax.ShapeDtypeStruct((8, rows), F32)],
        compiler_params=_params(("parallel",), VMEM_LIMIT),
        name="attn_out_router",
    )(o, x, w_o, g, b, w_r)


def _dispatch_kernel(pos_ref, x_ref, xs_in, xs_out, sem):
    del xs_in
    n = ROW_TILE
    tiles = x_ref.shape[0] // n

    def issue(r, carry):
        for t in range(tiles):
            for k in range(TOP_K_EXPERTS):
                slot_row = pos_ref[(t * TOP_K_EXPERTS + k) * n + r]
                pltpu.make_async_copy(x_ref.at[pl.ds(t * n + r, 1)], xs_out.at[pl.ds(slot_row, 1)],
                                      sem).start(priority=k % 2)
        return carry
    lax.fori_loop(0, n, issue, 0, unroll=8)
    for _ in range(TOP_K_EXPERTS):
        pltpu.make_async_copy(x_ref, xs_out.at[pl.ds(0, tiles * n)], sem).wait()


def _dispatch(pos, x, xs, *, tiles_per_step):
    rows = x.shape[0]
    block_rows = tiles_per_step * ROW_TILE
    return pl.pallas_call(
        _dispatch_kernel,
        grid=(rows // block_rows,),
        in_specs=[pl.BlockSpec((TOP_K_EXPERTS * block_rows,), lambda i: (i,), memory_space=pltpu.SMEM),
                  pl.BlockSpec((block_rows, D_MODEL), lambda i: (i, 0)),
                  pl.BlockSpec(memory_space=pl.ANY)],
        out_specs=pl.BlockSpec(memory_space=pl.ANY),
        out_shape=jax.ShapeDtypeStruct(xs.shape, xs.dtype),
        scratch_shapes=[pltpu.SemaphoreType.DMA(())],
        input_output_aliases={2: 0},
        compiler_params=_params(("arbitrary",)),
        name="moe_dispatch",
    )(pos, x, xs)


def _moe_kernel(te_ref, tv_ref, xs_ref, wg_ref, wu_ref, wd_ref, y_ref):
    i = pl.program_id(0)

    @pl.when(tv_ref[i] == 1)
    def _():
        y_ref[...] = _swiglu_body(xs_ref[...].astype(BF16), wg_ref[...], wu_ref[...], wd_ref[...])

    @pl.when(tv_ref[i] == 0)
    def _():
        y_ref[...] = jnp.zeros_like(y_ref)


def _moe(tile_expert, tile_valid, xs, wg, wu, wd):
    n_tiles = tile_expert.shape[0]
    d_ff = wg.shape[2]
    tile_spec = pl.BlockSpec((MOE_TILE, D_MODEL), lambda i, te, tv: (i, 0))
    return pl.pallas_call(
        _moe_kernel,
        grid_spec=pltpu.PrefetchScalarGridSpec(
            num_scalar_prefetch=2,
            grid=(n_tiles,),
            in_specs=[tile_spec,
                      pl.BlockSpec((None, D_MODEL, d_ff), lambda i, te, tv: (te[i], 0, 0)),
                      pl.BlockSpec((None, D_MODEL, d_ff), lambda i, te, tv: (te[i], 0, 0)),
                      pl.BlockSpec((None, d_ff, D_MODEL), lambda i, te, tv: (te[i], 0, 0))],
            out_specs=tile_spec,
        ),
        out_shape=jax.ShapeDtypeStruct((n_tiles * MOE_TILE, D_MODEL), F32),
        compiler_params=_params(("arbitrary",), VMEM_LIMIT),
        name="moe_experts",
    )(tile_expert, tile_valid, xs, wg, wu, wd)


COMBINE_SLOTS = 3


def _combine_kernel(pos_ref, pos1_ref, pos2_ref, y_hbm, x_ref, tw_ref, g_ref, b_ref, o_ref, ybuf, sem, *, n_steps):
    i = pl.program_id(0)
    slot = lax.rem(i, COMBINE_SLOTS)
    n = x_ref.shape[0]
    n_rows = TOP_K_EXPERTS * n

    def gather(idx_ref, sl):
        for r in range(n_rows):
            pltpu.make_async_copy(y_hbm.at[pl.ds(idx_ref[r], 1)], ybuf.at[sl, pl.ds(r, 1)],
                                  sem.at[sl]).start(priority=r % 2)

    def wait_rows(sl):
        pltpu.make_async_copy(y_hbm.at[pl.ds(0, n_rows)], ybuf.at[sl], sem.at[sl]).wait()

    @pl.when(i == 0)
    def _():
        gather(pos_ref, 0)
        gather(pos1_ref, 1)

    wait_rows(slot)
    gather(pos2_ref, lax.rem(i + 2, COMBINE_SLOTS))
    tw = tw_ref[...]
    moe = tw[:, 0:1] * ybuf[slot, :n, :] + tw[:, 1:2] * ybuf[slot, n:, :]
    o_ref[...] = _ln(ALPHA * x_ref[...] + moe, g_ref[...], b_ref[...])

    @pl.when(i == n_steps - 1)
    def _():
        wait_rows(lax.rem(i + 1, COMBINE_SLOTS))
        wait_rows(lax.rem(i + 2, COMBINE_SLOTS))


def _combine(pos, y_sorted, x3, tw, g, b):
    rows = x3.shape[0]
    n_steps = rows // ROW_TILE
    row_spec = pl.BlockSpec((ROW_TILE, D_MODEL), lambda i: (i, 0))

    def idx_spec(ahead):
        return pl.BlockSpec((TOP_K_EXPERTS * ROW_TILE,), lambda i: (jnp.minimum(i + ahead, n_steps - 1),),
                            memory_space=pltpu.SMEM)

    return pl.pallas_call(
        functools.partial(_combine_kernel, n_steps=n_steps),
        grid=(n_steps,),
        in_specs=[idx_spec(0), idx_spec(1), idx_spec(2),
                  pl.BlockSpec(memory_space=pl.ANY), row_spec,
                  pl.BlockSpec((ROW_TILE, TOP_K_EXPERTS), lambda i: (i, 0)),
                  _resident(g.shape), _resident(b.shape)],
        out_specs=row_spec,
        out_shape=jax.ShapeDtypeStruct((rows, D_MODEL), F32),
        scratch_shapes=[pltpu.VMEM((COMBINE_SLOTS, TOP_K_EXPERTS * ROW_TILE, D_MODEL), F32),
                        pltpu.SemaphoreType.DMA((COMBINE_SLOTS,))],
        compiler_params=_params(("arbitrary",), VMEM_LIMIT),
        name="moe_combine",
    )(pos, pos, pos, y_sorted, x3, tw, g, b)


def _route(top_i, n_tiles):
    rows = top_i.shape[1]
    e_flat = top_i.reshape(-1)
    experts = jnp.arange(N_EXPERTS, dtype=jnp.int32)
    onehot = (e_flat[None, :] == experts[:, None]).astype(jnp.int32)
    csum = jnp.cumsum(onehot, axis=1)
    rank = jnp.sum(onehot * csum, axis=0) - 1
    tiles_e = (csum[:, -1] + MOE_TILE - 1) // MOE_TILE
    tile_end = jnp.cumsum(tiles_e)
    tile_start = tile_end - tiles_e
    pos = jnp.sum(onehot * tile_start[:, None], axis=0) * MOE_TILE + rank
    tile_id = jnp.arange(n_tiles, dtype=jnp.int32)
    tile_expert = jnp.sum((tile_id[:, None] >= tile_end[None, :]).astype(jnp.int32), axis=1)
    tile_expert = jnp.minimum(tile_expert, N_EXPERTS - 1)
    tile_valid = (tile_id < tile_end[-1]).astype(jnp.int32)
    pos_tiles = pos.reshape(TOP_K_EXPERTS, rows // ROW_TILE, ROW_TILE).transpose(1, 0, 2).reshape(-1)
    return tile_expert, tile_valid, pos_tiles.astype(jnp.int32)


def kernel(x_prompt, x_sample, cache_k, cache_v, page_table, a_w_in, a_ln_g, a_ln_b, a_w_s, a_b_s, a_w_out,
           b_w_qkv, b_w_o, ffn_w_gate, ffn_w_up, ffn_w_down, moe_w_router, moe_w_gate, moe_w_up, moe_w_down,
           ln1_g, ln1_b, ln2_g, ln2_b):
    bsz, seq, _ = x_prompt.shape
    bd, t_new, _ = x_sample.shape
    rows_p, rows_s = bsz * seq, bd * t_new
    assert rows_s == CHUNK and CHUNK % t_new == 0 and seq % MOBA_BLOCK == 0
    n_pages = page_table.shape[1]
    past = n_pages * PAGE_SIZE
    assert past % MOBA_BLOCK == 0
    n_full = past // MOBA_BLOCK
    assert MOBA_TOPK <= n_full <= LANES

    xp = x_prompt.reshape(rows_p, D_MODEL)
    xs = x_sample.reshape(rows_s, D_MODEL)
    row = lambda a: a.reshape(1, -1)

    w_in = a_w_in[0].astype(BF16)
    w_out = a_w_out[0].astype(BF16)
    tril = jnp.tril(jnp.ones((CHUNK, CHUNK), bool))
    wmix_p = jnp.where(tril, a_w_s[0], 0.0).astype(BF16)
    bmix_p = jnp.repeat(a_b_s[0].T, LANES, axis=1)
    w_small = jnp.where(tril[:t_new, :t_new], a_w_s[0][:, :t_new, :t_new], 0.0)
    wmix_s = jnp.einsum("ab,gij->gaibj", jnp.eye(bd, dtype=F32), w_small).reshape(
        A_GROUPS, rows_s, rows_s).astype(BF16)
    bmix_s = jnp.repeat(jnp.tile(a_b_s[0][:, :t_new], (1, bd)).T, LANES, axis=1)
    gm_args = (w_in, row(a_ln_g[0]), row(a_ln_b[0]))
    gm_tail = (w_out, row(ln1_g[0]), row(ln1_b[0]))
    (xp,) = _gmlp(xp, *gm_args, wmix_p, bmix_p, *gm_tail, tile=512, emit_v=False)
    xs, v_rows = _gmlp(xs, *gm_args, wmix_s, bmix_s, *gm_tail, tile=rows_s, emit_v=True)

    ffn = (ffn_w_gate[0].astype(BF16), ffn_w_up[0].astype(BF16), ffn_w_down[0].astype(BF16),
           row(ln2_g[0]), row(ln2_b[0]))
    xp = _swiglu(xp, *ffn, tile=512)
    xs = _swiglu(xs, *ffn, tile=rows_s)

    w_qkv = b_w_qkv[0].astype(BF16)
    qkv_p, k_p, v_p, kmean = _qkv_prompt(xp, w_qkv, bsz, seq)
    o_p, ksum = _moba_prompt(qkv_p, kmean.reshape(bsz, seq // MOBA_BLOCK, D_MODEL), page_table, cache_k[0],
                             bsz, seq)

    qkv_s = _qkv_sample(xs, w_qkv).reshape(bd, t_new, 3 * D_MODEL)
    sel = _sample_select(qkv_s, ksum, n_full)[:, :, :t_new, :MOBA_TOPK]
    logical = (sel[..., None] * PAGES_PER_BLOCK
               + jnp.arange(PAGES_PER_BLOCK, dtype=jnp.int32)).reshape(bd, -1)
    phys = jnp.take_along_axis(page_table, logical, axis=1).reshape(-1)
    o_s = _sample_attn(phys, qkv_s, cache_k[0], cache_v[0]).reshape(rows_s, D_MODEL)
    kv_s = qkv_s.reshape(bd, t_new, 3, N_HEADS, HEAD_DIM)
    k_s = jnp.transpose(kv_s[:, :, 1], (0, 2, 1, 3))
    v_s = jnp.transpose(kv_s[:, :, 2], (0, 2, 1, 3))

    w_o = b_w_o[0].astype(BF16)
    w_r = jnp.pad(moe_w_router[0].T, ((0, 16 - N_EXPERTS), (0, 0))).astype(BF16)
    tail = (w_o, row(ln1_g[1]), row(ln1_b[1]), w_r)
    x3_p, ti_p, tw_p = _attn_out(o_p, xp, *tail, tile=512)
    x3_s, ti_s, tw_s = _attn_out(o_s, xs, *tail, tile=rows_s)

    top_i = jnp.concatenate([ti_p[:TOP_K_EXPERTS], ti_s[:TOP_K_EXPERTS]], axis=1)
    n_slots_used = TOP_K_EXPERTS * (rows_p + rows_s)
    n_tiles = n_slots_used // MOE_TILE + N_EXPERTS
    tile_expert, tile_valid, pos = _route(top_i, n_tiles)
    pos_p, pos_s = pos[:TOP_K_EXPERTS * rows_p], pos[TOP_K_EXPERTS * rows_p:]
    slots = jnp.zeros((n_tiles * MOE_TILE, D_MODEL), F32)
    slots = _dispatch(pos_p, x3_p, slots, tiles_per_step=2)
    slots = _dispatch(pos_s, x3_s, slots, tiles_per_step=1)
    y_slots = _moe(tile_expert, tile_valid, slots, moe_w_gate[0].astype(BF16),
                   moe_w_up[0].astype(BF16), moe_w_down[0].astype(BF16))
    ln2 = (row(ln2_g[1]), row(ln2_b[1]))
    y_prompt = _combine(pos_p, y_slots, x3_p, tw_p[:TOP_K_EXPERTS].T, *ln2).reshape(bsz, seq, D_MODEL)
    y_sample = _combine(pos_s, y_slots, x3_s, tw_s[:TOP_K_EXPERTS].T, *ln2).reshape(bd, t_new, D_MODEL)

    return (y_prompt, y_sample, v_rows.reshape(1, bd, t_new, D_MODEL), k_p[None], v_p[None],
            k_s[None], v_s[None])
```

```python
import functools

import jax
import jax.numpy as jnp
from jax import lax
from jax.experimental import pallas as pl
from jax.experimental.pallas import tpu as pltpu

F32 = jnp.float32
BF16 = jnp.bfloat16

D_MODEL = 1024
DEPTH = 2
CHUNK = 128
A_GROUPS = 8
N_HEADS = 8
HEAD_DIM = 128
MOBA_BLOCK = 256
MOBA_TOPK = 3
PAGE_SIZE = 128
PAGES_PER_BLOCK = MOBA_BLOCK // PAGE_SIZE
N_EXPERTS = 8
TOP_K_EXPERTS = 2
ATTN_SCALE = HEAD_DIM ** -0.5
EXP2_SCALE = ATTN_SCALE * 1.4426950408889634
ALPHA = (2 * DEPTH) ** 0.25
LN_EPS = 1e-5
INV_SQRT2 = 0.7071067811865476
NEG_INF = float("-inf")

LANES = 128
MOE_TILE = 256
ROW_TILE = 128
VMEM_LIMIT = 56 * 1024 * 1024

_NT_DIMS = (((1,), (1,)), ((), ()))


def _ln(x, g, b):
    mu = jnp.mean(x, axis=-1, keepdims=True)
    xc = x - mu
    var = jnp.mean(xc * xc, axis=-1, keepdims=True)
    return xc * lax.rsqrt(var + LN_EPS) * g + b


def _split_bf16(a):
    hi = a.astype(BF16)
    return hi, (a - hi.astype(F32)).astype(BF16)


def _resident(shape):
    nd = len(shape)
    return pl.BlockSpec(shape, lambda *_: (0,) * nd, pipeline_mode=pl.Buffered(1))


def _params(sem, vmem=None):
    return pltpu.CompilerParams(dimension_semantics=sem, vmem_limit_bytes=vmem)


def _gmlp_kernel(x_ref, win_ref, lng_ref, lnb_ref, wmix_ref, bmix_ref, wout_ref, g1_ref, b1_ref,
                 *out_refs, n_chunks, emit_v):
    group = min(n_chunks, 2) * CHUNK
    spans = [(r, r + group) for r in range(0, n_chunks * CHUNK, group)]
    hs = [jnp.dot(x_ref[a:b, :].astype(BF16), win_ref[...], preferred_element_type=F32) for a, b in spans]
    for (a, b), h in zip(spans, hs):
        h = 0.5 * h * (1.0 + lax.erf(h * INV_SQRT2))
        u = h[:, :D_MODEL]
        v = _ln(h[:, D_MODEL:], lng_ref[...], lnb_ref[...])
        if emit_v:
            out_refs[1][a:b, :] = v
        vb = v.astype(BF16)
        rows = []
        for c in range(group // CHUNK):
            cols = []
            for g in range(A_GROUPS):
                vg = vb[c * CHUNK:(c + 1) * CHUNK, g * LANES:(g + 1) * LANES]
                cols.append(jnp.dot(wmix_ref[g], vg, preferred_element_type=F32))
            rows.append(jnp.concatenate(cols, axis=1) + bmix_ref[...])
        z = rows[0] if len(rows) == 1 else jnp.concatenate(rows, axis=0)
        y = jnp.dot((u * z).astype(BF16), wout_ref[...], preferred_element_type=F32)
        out_refs[0][a:b, :] = _ln(ALPHA * x_ref[a:b, :] + y, g1_ref[...], b1_ref[...])


def _gmlp(x, w_in, ln_g, ln_b, wmix, bmix, w_out, g1, b1, *, tile, emit_v):
    rows = x.shape[0]
    n_chunks = tile // CHUNK
    row_spec = pl.BlockSpec((tile, D_MODEL), lambda i: (i, 0))
    out_shape = [jax.ShapeDtypeStruct((rows, D_MODEL), F32)]
    out_specs = [row_spec]
    if emit_v:
        out_shape.append(jax.ShapeDtypeStruct((rows, D_MODEL), F32))
        out_specs.append(row_spec)
    return pl.pallas_call(
        functools.partial(_gmlp_kernel, n_chunks=n_chunks, emit_v=emit_v),
        grid=(rows // tile,),
        in_specs=[row_spec, _resident(w_in.shape), _resident(ln_g.shape), _resident(ln_b.shape),
                  _resident(wmix.shape), _resident(bmix.shape), _resident(w_out.shape),
                  _resident(g1.shape), _resident(b1.shape)],
        out_specs=out_specs,
        out_shape=out_shape,
        compiler_params=_params(("parallel",), VMEM_LIMIT),
        name="gmlp_mixer",
    )(x, w_in, ln_g, ln_b, wmix, bmix, w_out, g1, b1)


def _swiglu_body(xb, wg, wu, wd):
    a = jnp.dot(xb, wg, preferred_element_type=F32)
    up = jnp.dot(xb, wu, preferred_element_type=F32)
    h = (a * jax.nn.sigmoid(a)) * up
    return jnp.dot(h.astype(BF16), wd, preferred_element_type=F32)


def _swiglu_kernel(x_ref, wg_ref, wu_ref, wd_ref, g_ref, b_ref, o_ref):
    x = x_ref[...]
    f = _swiglu_body(x.astype(BF16), wg_ref[...], wu_ref[...], wd_ref[...])
    o_ref[...] = _ln(ALPHA * x + f, g_ref[...], b_ref[...])


def _swiglu(x, wg, wu, wd, g, b, *, tile):
    rows = x.shape[0]
    row_spec = pl.BlockSpec((tile, D_MODEL), lambda i: (i, 0))
    return pl.pallas_call(
        _swiglu_kernel,
        grid=(rows // tile,),
        in_specs=[row_spec, _resident(wg.shape), _resident(wu.shape), _resident(wd.shape),
                  _resident(g.shape), _resident(b.shape)],
        out_specs=row_spec,
        out_shape=jax.ShapeDtypeStruct((rows, D_MODEL), F32),
        compiler_params=_params(("parallel",), VMEM_LIMIT),
        name="dense_swiglu",
    )(x, wg, wu, wd, g, b)


QKV_BLOCKS_PER_STEP = 2


def _qkv_prompt_kernel(x_ref, w_ref, qkv_ref, k_ref, v_ref, kmean_ref):
    r = jnp.dot(x_ref[...].astype(BF16), w_ref[...], preferred_element_type=F32)
    qkv_ref[...] = r.astype(BF16)
    for h in range(N_HEADS):
        k_ref[h] = r[:, D_MODEL + h * HEAD_DIM:D_MODEL + (h + 1) * HEAD_DIM]
        v_ref[h] = r[:, 2 * D_MODEL + h * HEAD_DIM:2 * D_MODEL + (h + 1) * HEAD_DIM]
    for n in range(QKV_BLOCKS_PER_STEP):
        kmean_ref[n] = jnp.mean(r[n * MOBA_BLOCK:(n + 1) * MOBA_BLOCK, D_MODEL:2 * D_MODEL], axis=0, keepdims=True)


def _qkv_prompt(x, w_qkv, bsz, seq):
    nb = seq // MOBA_BLOCK
    tile = QKV_BLOCKS_PER_STEP * MOBA_BLOCK
    steps_per_seq = seq // tile
    kv_shape = jax.ShapeDtypeStruct((bsz, N_HEADS, seq, HEAD_DIM), F32)
    kv_spec = pl.BlockSpec((None, N_HEADS, tile, HEAD_DIM), lambda i: (i // steps_per_seq, 0, i % steps_per_seq, 0))
    return pl.pallas_call(
        _qkv_prompt_kernel,
        grid=(bsz * steps_per_seq,),
        in_specs=[pl.BlockSpec((tile, D_MODEL), lambda i: (i, 0)), _resident(w_qkv.shape)],
        out_specs=[pl.BlockSpec((tile, 3 * D_MODEL), lambda i: (i, 0)), kv_spec, kv_spec,
                   pl.BlockSpec((QKV_BLOCKS_PER_STEP, 1, D_MODEL), lambda i: (i, 0, 0))],
        out_shape=[jax.ShapeDtypeStruct((bsz * seq, 3 * D_MODEL), BF16), kv_shape, kv_shape,
                   jax.ShapeDtypeStruct((bsz * nb, 1, D_MODEL), F32)],
        compiler_params=_params(("parallel",), VMEM_LIMIT),
        name="qkv_prompt",
    )(x, w_qkv)


def _qkv_sample_kernel(x_ref, w_ref, o_ref):
    o_ref[...] = jnp.dot(x_ref[...].astype(BF16), w_ref[...], preferred_element_type=F32)


def _qkv_sample(x, w_qkv):
    rows = x.shape[0]
    return pl.pallas_call(
        _qkv_sample_kernel,
        grid=(1,),
        in_specs=[_resident(x.shape), _resident(w_qkv.shape)],
        out_specs=pl.BlockSpec((rows, 3 * D_MODEL), lambda i: (0, 0)),
        out_shape=jax.ShapeDtypeStruct((rows, 3 * D_MODEL), F32),
        compiler_params=_params(("arbitrary",), VMEM_LIMIT),
        name="qkv_sample",
    )(x, w_qkv)


PAIRS_PER_ROUND = 4
STREAM_SLOTS = 4


def _first_argmax(g, idx, size, axis):
    m = jnp.max(g, axis=axis, keepdims=True)
    return jnp.min(jnp.where(g == m, idx, size), axis=axis, keepdims=True)


def _moba_prompt_kernel(it_ref, jt_ref, pt_ref, q_ref, k_ref, v_ref, km_ref, kpool, o_ref, ks_ref,
                        vt_sc, sel_sc, m_sc, l_sc, acc_sc, sa_sc, sb_sc, pbuf, psem,
                        *, n_blocks, n_own_rounds, n_rounds, n_steps):
    nb = n_blocks
    km_hi, km_lo = _split_bf16(km_ref[...])

    step = pl.program_id(0) * N_HEADS + pl.program_id(1)
    n_past_trips = (n_rounds - n_own_rounds) // 2
    chunks_per_step = 1 + n_past_trips
    n_slots, pages_per_chunk = pbuf.shape[:2]
    blocks_per_chunk = pages_per_chunk // PAGES_PER_BLOCK
    last_chunk = n_steps * chunks_per_step - 1

    def chunk_copies(chunk, slot):
        seq = chunk // chunks_per_step
        first = (chunk - seq * chunks_per_step) * pages_per_chunk
        return [pltpu.make_async_copy(kpool.at[pt_ref[seq, first + m]], pbuf.at[slot, m], psem.at[slot])
                for m in range(pages_per_chunk)]

    def stream_chunk(c):
        chunk = step * chunks_per_step + c
        slot = lax.rem(chunk, n_slots)
        for cp in chunk_copies(chunk, slot):
            cp.wait()
        ahead = chunk + n_slots - 1
        for cp in chunk_copies(jnp.minimum(ahead, last_chunk), lax.rem(ahead, n_slots)):
            cp.start()
        for n in range(blocks_per_chunk):
            tot = jnp.sum(pbuf[slot, PAGES_PER_BLOCK * n], axis=1)
            for m in range(1, PAGES_PER_BLOCK):
                tot = tot + jnp.sum(pbuf[slot, PAGES_PER_BLOCK * n + m], axis=1)
            ks_ref[:, pl.ds(c * blocks_per_chunk + n, 1), :] = tot[:, None, :]

    @pl.when(step == 0)
    def _():
        for g in range(n_slots - 1):
            for cp in chunk_copies(g, g):
                cp.start()

    ks_ref[...] = jnp.zeros_like(ks_ref)

    def rows_of(block):
        return pl.ds(pl.multiple_of(block * MOBA_BLOCK, MOBA_BLOCK), MOBA_BLOCK)

    def transpose_v(j, carry):
        vt_sc[j] = v_ref[rows_of(j), :].astype(F32).T.astype(BF16)
        return carry
    lax.fori_loop(0, nb, transpose_v, 0, unroll=4)

    blk_id = lax.broadcasted_iota(jnp.int32, (nb, MOBA_BLOCK), 0)
    key_id = lax.broadcasted_iota(jnp.int32, (MOBA_BLOCK, MOBA_BLOCK), 0)
    qry_id = lax.broadcasted_iota(jnp.int32, (MOBA_BLOCK, MOBA_BLOCK), 1)

    def select_blocks(r, carry):
        tiles = [r * PAIRS_PER_ROUND + u for u in range(PAIRS_PER_ROUND)]
        gates = []
        for i in tiles:
            q = q_ref[rows_of(i), :]
            gates.append(lax.dot_general(km_hi, q, _NT_DIMS, preferred_element_type=F32)
                         + lax.dot_general(km_lo, q, _NT_DIMS, preferred_element_type=F32))
        for i, gate in zip(tiles, gates):
            past = blk_id < i
            g = jnp.where(past, gate, NEG_INF)
            sel = jnp.zeros((nb, MOBA_BLOCK), F32)
            for _ in range(MOBA_TOPK):
                pick = blk_id == _first_argmax(g, blk_id, nb, 0)
                sel = jnp.where(pick, 1.0, sel)
                g = jnp.where(pick, NEG_INF, g)
            sel = jnp.where(past, sel, 0.0)
            for j in range(nb):
                sel_sc[i, j] = sel[j:j + 1, :]
        return carry
    lax.fori_loop(0, n_own_rounds, select_blocks, 0)

    def pair_ids(c):
        return [(it_ref[c * PAIRS_PER_ROUND + u], jt_ref[c * PAIRS_PER_ROUND + u])
                for u in range(PAIRS_PER_ROUND)]

    def score_round(c, dst):
        for u, (i, j) in enumerate(pair_ids(c)):
            dst[u] = lax.dot_general(k_ref[rows_of(j), :], q_ref[rows_of(i), :], _NT_DIMS,
                                     preferred_element_type=F32)

    def softmax_own(c, src):
        ids = pair_ids(c)
        soft = []
        for u in range(PAIRS_PER_ROUND):
            s = jnp.where(key_id <= qry_id, src[u], NEG_INF)
            m = jnp.max(s, axis=0, keepdims=True)
            p = jnp.exp2((s - m) * EXP2_SCALE)
            soft.append((m, jnp.sum(p, axis=0, keepdims=True), p.astype(BF16)))
        pv = [jnp.dot(vt_sc[j], p, preferred_element_type=F32) for (_, j), (_, _, p) in zip(ids, soft)]
        for (i, _), (m, l, _), o in zip(ids, soft, pv):
            m_sc[i] = m
            l_sc[i] = l
            acc_sc[i] = o

    def softmax_past(c, src):
        ids = pair_ids(c)
        old = [(m_sc[i], l_sc[i], acc_sc[i]) for i, _ in ids]
        soft = []
        for u, ((i, j), (m_old, l_old, _)) in enumerate(zip(ids, old)):
            s = jnp.where(sel_sc[i, j] > 0.0, src[u], NEG_INF)
            m_new = jnp.maximum(m_old, jnp.max(s, axis=0, keepdims=True))
            a = jnp.exp2((m_old - m_new) * EXP2_SCALE)
            p = jnp.exp2((s - m_new) * EXP2_SCALE)
            soft.append((m_new, a, a * l_old + jnp.sum(p, axis=0, keepdims=True), p.astype(BF16)))
        pv = [jnp.dot(vt_sc[j], p, preferred_element_type=F32) for (_, j), (_, _, _, p) in zip(ids, soft)]
        for (i, _), (m_new, a, l_new, _), (_, _, acc_old), o in zip(ids, soft, old, pv):
            m_sc[i] = m_new
            l_sc[i] = l_new
            acc_sc[i] = a * acc_old + o

    def two_rounds(softmax_round, stream):
        def body(h, carry):
            c = 2 * h
            score_round(c + 1, sb_sc)
            softmax_round(c, sa_sc)
            score_round(jnp.minimum(c + 2, n_rounds - 1), sa_sc)
            softmax_round(c + 1, sb_sc)
            if stream:
                stream_chunk(h - n_own_rounds // 2 + 1)
            return carry
        return body

    score_round(0, sa_sc)
    lax.fori_loop(0, n_own_rounds // 2, two_rounds(softmax_own, False), 0)
    stream_chunk(0)
    lax.fori_loop(n_own_rounds // 2, n_rounds // 2, two_rounds(softmax_past, True), 0)

    @pl.when(step == n_steps - 1)
    def _():
        for g in range(1, n_slots):
            for cp in chunk_copies(last_chunk, (last_chunk + g) % n_slots):
                cp.wait()

    def finish(i, carry):
        o_ref[rows_of(i), :] = (acc_sc[i] / l_sc[i]).T.astype(o_ref.dtype)
        return carry
    lax.fori_loop(0, nb, finish, 0, unroll=4)


def _pair_rounds(nb):
    assert nb % (2 * PAIRS_PER_ROUND) == 0
    rounds = [[(i, i) for i in range(r, r + PAIRS_PER_ROUND)] for r in range(0, nb, PAIRS_PER_ROUND)]
    n_own = len(rounds)
    remaining = {i: list(range(i)) for i in range(1, nb)}
    while any(remaining.values()):
        tiles = sorted((i for i in remaining if remaining[i]), key=lambda i: -len(remaining[i]))
        rounds.append([(i, remaining[i].pop()) for i in tiles[:PAIRS_PER_ROUND]])
    assert all(len(r) == PAIRS_PER_ROUND for r in rounds) and len(rounds) % 2 == 0
    return rounds, n_own


def _moba_prompt(qkv, kmean, page_table, k_pool, bsz, seq):
    nb = seq // MOBA_BLOCK
    rounds, n_own = _pair_rounds(nb)
    i_tab = jnp.array([i for r in rounds for i, _ in r], jnp.int32)
    j_tab = jnp.array([j for r in rounds for _, j in r], jnp.int32)
    bd, n_pages = page_table.shape
    n_steps = bsz * N_HEADS
    chunks_per_step = 1 + (len(rounds) - n_own) // 2
    assert bd == n_steps and n_pages % (chunks_per_step * PAGES_PER_BLOCK) == 0
    assert n_steps * chunks_per_step >= STREAM_SLOTS
    pages_per_chunk = n_pages // chunks_per_step
    assert n_pages // PAGES_PER_BLOCK <= LANES

    def head_spec(off):
        return pl.BlockSpec((seq, HEAD_DIM), lambda b, h, *_: (b, off + h))

    return pl.pallas_call(
        functools.partial(_moba_prompt_kernel, n_blocks=nb, n_own_rounds=n_own, n_rounds=len(rounds),
                          n_steps=n_steps),
        grid_spec=pltpu.PrefetchScalarGridSpec(
            num_scalar_prefetch=3,
            grid=(bsz, N_HEADS),
            in_specs=[head_spec(0), head_spec(N_HEADS), head_spec(2 * N_HEADS),
                      pl.BlockSpec((None, nb, HEAD_DIM), lambda b, h, *_: (b, 0, h)),
                      pl.BlockSpec(memory_space=pl.ANY)],
            out_specs=[head_spec(0),
                       pl.BlockSpec((None, N_HEADS, LANES, HEAD_DIM),
                                    lambda b, h, *_: (b * N_HEADS + h, 0, 0, 0))],
            scratch_shapes=[pltpu.VMEM((nb, HEAD_DIM, MOBA_BLOCK), BF16),
                            pltpu.VMEM((nb, nb, 1, MOBA_BLOCK), F32),
                            pltpu.VMEM((nb, 1, MOBA_BLOCK), F32),
                            pltpu.VMEM((nb, 1, MOBA_BLOCK), F32),
                            pltpu.VMEM((nb, HEAD_DIM, MOBA_BLOCK), F32),
                            pltpu.VMEM((PAIRS_PER_ROUND, MOBA_BLOCK, MOBA_BLOCK), F32),
                            pltpu.VMEM((PAIRS_PER_ROUND, MOBA_BLOCK, MOBA_BLOCK), F32),
                            pltpu.VMEM((STREAM_SLOTS, pages_per_chunk, N_HEADS, PAGE_SIZE, HEAD_DIM), F32),
                            pltpu.SemaphoreType.DMA((STREAM_SLOTS,))],
        ),
        out_shape=[jax.ShapeDtypeStruct((bsz * seq, D_MODEL), BF16),
                   jax.ShapeDtypeStruct((bd, N_HEADS, LANES, HEAD_DIM), F32)],
        compiler_params=_params(("arbitrary", "arbitrary"), VMEM_LIMIT),
        name="moba_prompt",
    )(i_tab, j_tab, page_table, qkv, qkv, qkv, kmean, k_pool)


def _sample_select_kernel(q_ref, ks_ref, sel_ref, *, n_full):
    t_new = q_ref.shape[0]
    inv = 1.0 / MOBA_BLOCK
    gates = []
    for h in range(N_HEADS):
        qh = jnp.concatenate([q_ref[:, h * HEAD_DIM:(h + 1) * HEAD_DIM],
                              jnp.zeros((8 - t_new, HEAD_DIM), F32)], axis=0)
        q_hi, q_lo = _split_bf16(qh)
        k_hi, k_lo = _split_bf16(ks_ref[h] * inv)
        gates.append(lax.dot_general(q_hi, k_hi, _NT_DIMS, preferred_element_type=F32)
                     + lax.dot_general(q_hi, k_lo, _NT_DIMS, preferred_element_type=F32)
                     + lax.dot_general(q_lo, k_hi, _NT_DIMS, preferred_element_type=F32))
    g = jnp.concatenate(gates, axis=0)
    col = lax.broadcasted_iota(jnp.int32, g.shape, 1)
    g = jnp.where(col < n_full, g, NEG_INF)
    out = jnp.zeros(g.shape, jnp.int32)
    for r in range(MOBA_TOPK):
        idx = _first_argmax(g, col, LANES, 1)
        out = jnp.where(col == r, idx, out)
        g = jnp.where(col == idx, NEG_INF, g)
    sel_ref[...] = out.reshape(sel_ref.shape)


def _sample_select(qkv_s, ksum, n_full):
    bd, t_new, _ = qkv_s.shape
    return pl.pallas_call(
        functools.partial(_sample_select_kernel, n_full=n_full),
        grid=(bd,),
        in_specs=[pl.BlockSpec((None, t_new, D_MODEL), lambda b: (b, 0, 0)),
                  pl.BlockSpec((None, N_HEADS, LANES, HEAD_DIM), lambda b: (b, 0, 0, 0))],
        out_specs=pl.BlockSpec((None, N_HEADS, 8, LANES), lambda b: (b, 0, 0, 0)),
        out_shape=jax.ShapeDtypeStruct((bd, N_HEADS, 8, LANES), jnp.int32),
        compiler_params=_params(("parallel",)),
        name="sample_select",
    )(qkv_s, ksum)


N_GATHER = MOBA_TOPK * PAGES_PER_BLOCK


def _sample_attn_kernel(phys_ref, q_ref, kn_ref, vn_ref, kpool, vpool, o_ref, kbuf, vbuf, sem,
                        *, t_new, n_steps):
    n_in = t_new * N_GATHER
    n_slots = kbuf.shape[0]
    ahead = n_slots - 1
    step = pl.program_id(0) * N_HEADS + pl.program_id(1)
    slot = lax.rem(step, n_slots)

    def page_copies(st, sl):
        head = lax.rem(st, N_HEADS)
        out = []
        for m in range(n_in):
            page = phys_ref[st * n_in + m]
            out.append(pltpu.make_async_copy(kpool.at[page, head], kbuf.at[sl, m], sem.at[sl]))
            out.append(pltpu.make_async_copy(vpool.at[page, head], vbuf.at[sl, m], sem.at[sl]))
        return out

    @pl.when(step == 0)
    def _():
        for st in range(min(ahead, n_steps)):
            for c in page_copies(st, st):
                c.start()

    @pl.when(step + ahead < n_steps)
    def _():
        for c in page_copies(step + ahead, lax.rem(step + ahead, n_slots)):
            c.start()

    for c in page_copies(step, slot):
        c.wait()

    row = lax.broadcasted_iota(jnp.int32, (t_new, 1), 0)
    for t in range(t_new):
        q = q_ref[t:t + 1, :]
        pages = range(t * N_GATHER, (t + 1) * N_GATHER)
        s_own = jnp.sum(kn_ref[...] * q, axis=1, keepdims=True) * ATTN_SCALE
        s_own = jnp.where(row <= t, s_own, NEG_INF)
        s_past = [jnp.sum(kbuf[slot, m] * q, axis=1, keepdims=True) * ATTN_SCALE for m in pages]
        mx = jnp.max(s_own, axis=0, keepdims=True)
        for sp in s_past:
            mx = jnp.maximum(mx, jnp.max(sp, axis=0, keepdims=True))
        p_own = jnp.exp(s_own - mx)
        l = jnp.sum(p_own, axis=0, keepdims=True)
        o = jnp.sum(p_own * vn_ref[...], axis=0, keepdims=True)
        for sp, m in zip(s_past, pages):
            pp = jnp.exp(sp - mx)
            l = l + jnp.sum(pp, axis=0, keepdims=True)
            o = o + jnp.sum(pp * vbuf[slot, m], axis=0, keepdims=True)
        o_ref[t:t + 1, :] = o / l


def _sample_attn(phys, qkv_s, k_pool, v_pool):
    bd, t_new, _ = qkv_s.shape
    n_in = t_new * N_GATHER

    def head_spec(off):
        return pl.BlockSpec((None, t_new, HEAD_DIM), lambda b, h, ph: (b, 0, off + h))

    n_slots = 3
    page_buf = pltpu.VMEM((n_slots, n_in, PAGE_SIZE, HEAD_DIM), F32)
    return pl.pallas_call(
        functools.partial(_sample_attn_kernel, t_new=t_new, n_steps=bd * N_HEADS),
        grid_spec=pltpu.PrefetchScalarGridSpec(
            num_scalar_prefetch=1,
            grid=(bd, N_HEADS),
            in_specs=[head_spec(0), head_spec(N_HEADS), head_spec(2 * N_HEADS),
                      pl.BlockSpec(memory_space=pl.ANY), pl.BlockSpec(memory_space=pl.ANY)],
            out_specs=head_spec(0),
            scratch_shapes=[page_buf, page_buf, pltpu.SemaphoreType.DMA((n_slots,))],
        ),
        out_shape=jax.ShapeDtypeStruct((bd, t_new, D_MODEL), F32),
        compiler_params=_params(("arbitrary", "arbitrary"), VMEM_LIMIT),
        name="sample_attn",
    )(phys, qkv_s, qkv_s, qkv_s, k_pool, v_pool)


def _attn_out_kernel(o_ref, x_ref, wo_ref, g_ref, b_ref, wr_ref, x3_ref, ti_ref, tw_ref):
    m = jnp.dot(o_ref[...].astype(BF16), wo_ref[...], preferred_element_type=F32)
    x3 = _ln(ALPHA * x_ref[...] + m, g_ref[...], b_ref[...])
    x3_ref[...] = x3
    logits = lax.dot_general(wr_ref[...], x3.astype(BF16), _NT_DIMS, preferred_element_type=F32)
    n_rows = logits.shape[0]
    row = lax.broadcasted_iota(jnp.int32, logits.shape, 0)
    lg = jnp.where(row < N_EXPERTS, logits, NEG_INF)
    v1 = jnp.max(lg, axis=0, keepdims=True)
    i1 = jnp.min(jnp.where(lg == v1, row, n_rows), axis=0, keepdims=True)
    lg2 = jnp.where(row == i1, NEG_INF, lg)
    v2 = jnp.max(lg2, axis=0, keepdims=True)
    i2 = jnp.min(jnp.where(lg2 == v2, row, n_rows), axis=0, keepdims=True)
    e = jnp.exp(v2 - v1)
    w1 = 1.0 / (1.0 + e)
    w2 = e / (1.0 + e)
    out_row = lax.broadcasted_iota(jnp.int32, ti_ref.shape, 0)
    ti_ref[...] = jnp.where(out_row == 0, i1, jnp.where(out_row == 1, i2, 0))
    tw_ref[...] = jnp.where(out_row == 0, w1, jnp.where(out_row == 1, w2, 0.0))


def _attn_out(o, x, w_o, g, b, w_r, *, tile):
    rows = x.shape[0]
    row_spec = pl.BlockSpec((tile, D_MODEL), lambda i: (i, 0))
    top_spec = pl.BlockSpec((8, tile), lambda i: (0, i))
    return pl.pallas_call(
        _attn_out_kernel,
        grid=(rows // tile,),
        in_specs=[row_spec, row_spec, _resident(w_o.shape), _resident(g.shape), _resident(b.shape),
                  _resident(w_r.shape)],
        out_specs=[row_spec, top_spec, top_spec],
        out_shape=[jax.ShapeDtypeStruct((rows, D_MODEL), F32),
                   jax.ShapeDtypeStruct((8, rows), jnp.int32),
                   jax.ShapeDtypeStruct((8, rows), F32)],
        compiler_params=_params(("parallel",), VMEM_LIMIT),
        name="attn_out_router",
    )(o, x, w_o, g, b, w_r)


def _dispatch_kernel(pos_ref, x_ref, xs_in, xs_out, sem):
    del xs_in
    n = ROW_TILE
    tiles = x_ref.shape[0] // n

    for r in range(n):
        for t in range(tiles):
            for k in range(TOP_K_EXPERTS):
                slot_row = pos_ref[(t * TOP_K_EXPERTS + k) * n + r]
                pltpu.make_async_copy(x_ref.at[pl.ds(t * n + r, 1)], xs_out.at[pl.ds(slot_row, 1)],
                                      sem).start(priority=k % 2)
    for _ in range(TOP_K_EXPERTS):
        pltpu.make_async_copy(x_ref, xs_out.at[pl.ds(0, tiles * n)], sem).wait()


def _dispatch(pos, x, xs, *, tiles_per_step):
    rows = x.shape[0]
    block_rows = tiles_per_step * ROW_TILE
    return pl.pallas_call(
        _dispatch_kernel,
        grid=(rows // block_rows,),
        in_specs=[pl.BlockSpec((TOP_K_EXPERTS * block_rows,), lambda i: (i,), memory_space=pltpu.SMEM),
                  pl.BlockSpec((block_rows, D_MODEL), lambda i: (i, 0)),
                  pl.BlockSpec(memory_space=pl.ANY)],
        out_specs=pl.BlockSpec(memory_space=pl.ANY),
        out_shape=jax.ShapeDtypeStruct(xs.shape, xs.dtype),
        scratch_shapes=[pltpu.SemaphoreType.DMA(())],
        input_output_aliases={2: 0},
        compiler_params=_params(("arbitrary",)),
        name="moe_dispatch",
    )(pos, x, xs)


def _moe_kernel(te_ref, tv_ref, xs_ref, wg_ref, wu_ref, wd_ref, y_ref):
    i = pl.program_id(0)

    @pl.when(tv_ref[i] == 1)
    def _():
        y_ref[...] = _swiglu_body(xs_ref[...].astype(BF16), wg_ref[...], wu_ref[...], wd_ref[...])

    @pl.when(tv_ref[i] == 0)
    def _():
        y_ref[...] = jnp.zeros_like(y_ref)


def _moe(tile_expert, tile_valid, xs, wg, wu, wd):
    n_tiles = tile_expert.shape[0]
    d_ff = wg.shape[2]
    tile_spec = pl.BlockSpec((MOE_TILE, D_MODEL), lambda i, te, tv: (i, 0))
    return pl.pallas_call(
        _moe_kernel,
        grid_spec=pltpu.PrefetchScalarGridSpec(
            num_scalar_prefetch=2,
            grid=(n_tiles,),
            in_specs=[tile_spec,
                      pl.BlockSpec((None, D_MODEL, d_ff), lambda i, te, tv: (te[i], 0, 0)),
                      pl.BlockSpec((None, D_MODEL, d_ff), lambda i, te, tv: (te[i], 0, 0)),
                      pl.BlockSpec((None, d_ff, D_MODEL), lambda i, te, tv: (te[i], 0, 0))],
            out_specs=tile_spec,
        ),
        out_shape=jax.ShapeDtypeStruct((n_tiles * MOE_TILE, D_MODEL), F32),
        compiler_params=_params(("arbitrary",), VMEM_LIMIT),
        name="moe_experts",
    )(tile_expert, tile_valid, xs, wg, wu, wd)


COMBINE_SLOTS = 3


def _combine_kernel(pos_ref, pos1_ref, pos2_ref, y_hbm, x_ref, tw_ref, g_ref, b_ref, o_ref, ybuf, sem, *, n_steps):
    i = pl.program_id(0)
    slot = lax.rem(i, COMBINE_SLOTS)
    n = x_ref.shape[0]
    n_rows = TOP_K_EXPERTS * n

    def gather(idx_ref, sl):
        for r in range(n_rows):
            pltpu.make_async_copy(y_hbm.at[pl.ds(idx_ref[r], 1)], ybuf.at[sl, pl.ds(r, 1)],
                                  sem.at[sl]).start(priority=r % 2)

    def wait_rows(sl):
        pltpu.make_async_copy(y_hbm.at[pl.ds(0, n_rows)], ybuf.at[sl], sem.at[sl]).wait()

    @pl.when(i == 0)
    def _():
        gather(pos_ref, 0)
        gather(pos1_ref, 1)

    wait_rows(slot)
    gather(pos2_ref, lax.rem(i + 2, COMBINE_SLOTS))
    tw = tw_ref[...]
    moe = tw[:, 0:1] * ybuf[slot, :n, :] + tw[:, 1:2] * ybuf[slot, n:, :]
    o_ref[...] = _ln(ALPHA * x_ref[...] + moe, g_ref[...], b_ref[...])

    @pl.when(i == n_steps - 1)
    def _():
        wait_rows(lax.rem(i + 1, COMBINE_SLOTS))
        wait_rows(lax.rem(i + 2, COMBINE_SLOTS))


def _combine(pos, y_sorted, x3, tw, g, b):
    rows = x3.shape[0]
    n_steps = rows // ROW_TILE
    row_spec = pl.BlockSpec((ROW_TILE, D_MODEL), lambda i: (i, 0))

    def idx_spec(ahead):
        return pl.BlockSpec((TOP_K_EXPERTS * ROW_TILE,), lambda i: (jnp.minimum(i + ahead, n_steps - 1),),
                            memory_space=pltpu.SMEM)

    return pl.pallas_call(
        functools.partial(_combine_kernel, n_steps=n_steps),
        grid=(n_steps,),
        in_specs=[idx_spec(0), idx_spec(1), idx_spec(2),
                  pl.BlockSpec(memory_space=pl.ANY), row_spec,
                  pl.BlockSpec((ROW_TILE, TOP_K_EXPERTS), lambda i: (i, 0)),
                  _resident(g.shape), _resident(b.shape)],
        out_specs=row_spec,
        out_shape=jax.ShapeDtypeStruct((rows, D_MODEL), F32),
        scratch_shapes=[pltpu.VMEM((COMBINE_SLOTS, TOP_K_EXPERTS * ROW_TILE, D_MODEL), F32),
                        pltpu.SemaphoreType.DMA((COMBINE_SLOTS,))],
        compiler_params=_params(("arbitrary",), VMEM_LIMIT),
        name="moe_combine",
    )(pos, pos, pos, y_sorted, x3, tw, g, b)


def _route(top_i, n_tiles):
    rows = top_i.shape[1]
    e_flat = top_i.reshape(-1)
    experts = jnp.arange(N_EXPERTS, dtype=jnp.int32)
    onehot = (e_flat[None, :] == experts[:, None]).astype(jnp.int32)
    csum = jnp.cumsum(onehot, axis=1)
    rank = jnp.sum(onehot * csum, axis=0) - 1
    tiles_e = (csum[:, -1] + MOE_TILE - 1) // MOE_TILE
    tile_end = jnp.cumsum(tiles_e)
    tile_start = tile_end - tiles_e
    pos = jnp.sum(onehot * tile_start[:, None], axis=0) * MOE_TILE + rank
    tile_id = jnp.arange(n_tiles, dtype=jnp.int32)
    tile_expert = jnp.sum((tile_id[:, None] >= tile_end[None, :]).astype(jnp.int32), axis=1)
    tile_expert = jnp.minimum(tile_expert, N_EXPERTS - 1)
    tile_valid = (tile_id < tile_end[-1]).astype(jnp.int32)
    pos_tiles = pos.reshape(TOP_K_EXPERTS, rows // ROW_TILE, ROW_TILE).transpose(1, 0, 2).reshape(-1)
    return tile_expert, tile_valid, pos_tiles.astype(jnp.int32)


def kernel(x_prompt, x_sample, cache_k, cache_v, page_table, a_w_in, a_ln_g, a_ln_b, a_w_s, a_b_s, a_w_out,
           b_w_qkv, b_w_o, ffn_w_gate, ffn_w_up, ffn_w_down, moe_w_router, moe_w_gate, moe_w_up, moe_w_down,
           ln1_g, ln1_b, ln2_g, ln2_b):
    bsz, seq, _ = x_prompt.shape
    bd, t_new, _ = x_sample.shape
    rows_p, rows_s = bsz * seq, bd * t_new
    assert rows_s == CHUNK and CHUNK % t_new == 0 and seq % MOBA_BLOCK == 0
    n_pages = page_table.shape[1]
    past = n_pages * PAGE_SIZE
    assert past % MOBA_BLOCK == 0
    n_full = past // MOBA_BLOCK
    assert MOBA_TOPK <= n_full <= LANES

    xp = x_prompt.reshape(rows_p, D_MODEL)
    xs = x_sample.reshape(rows_s, D_MODEL)
    row = lambda a: a.reshape(1, -1)

    w_in = a_w_in[0].astype(BF16)
    w_out = a_w_out[0].astype(BF16)
    tril = jnp.tril(jnp.ones((CHUNK, CHUNK), bool))
    wmix_p = jnp.where(tril, a_w_s[0], 0.0).astype(BF16)
    bmix_p = jnp.repeat(a_b_s[0].T, LANES, axis=1)
    w_small = jnp.where(tril[:t_new, :t_new], a_w_s[0][:, :t_new, :t_new], 0.0)
    wmix_s = jnp.einsum("ab,gij->gaibj", jnp.eye(bd, dtype=F32), w_small).reshape(
        A_GROUPS, rows_s, rows_s).astype(BF16)
    bmix_s = jnp.repeat(jnp.tile(a_b_s[0][:, :t_new], (1, bd)).T, LANES, axis=1)
    gm_args = (w_in, row(a_ln_g[0]), row(a_ln_b[0]))
    gm_tail = (w_out, row(ln1_g[0]), row(ln1_b[0]))
    (xp,) = _gmlp(xp, *gm_args, wmix_p, bmix_p, *gm_tail, tile=1024, emit_v=False)
    xs, v_rows = _gmlp(xs, *gm_args, wmix_s, bmix_s, *gm_tail, tile=rows_s, emit_v=True)

    ffn = (ffn_w_gate[0].astype(BF16), ffn_w_up[0].astype(BF16), ffn_w_down[0].astype(BF16),
           row(ln2_g[0]), row(ln2_b[0]))
    xp = _swiglu(xp, *ffn, tile=512)
    xs = _swiglu(xs, *ffn, tile=rows_s)

    w_qkv = b_w_qkv[0].astype(BF16)
    qkv_p, k_p, v_p, kmean = _qkv_prompt(xp, w_qkv, bsz, seq)
    o_p, ksum = _moba_prompt(qkv_p, kmean.reshape(bsz, seq // MOBA_BLOCK, D_MODEL), page_table, cache_k[0],
                             bsz, seq)

    qkv_s = _qkv_sample(xs, w_qkv).reshape(bd, t_new, 3 * D_MODEL)
    sel = _sample_select(qkv_s, ksum, n_full)[:, :, :t_new, :MOBA_TOPK]
    logical = (sel[..., None] * PAGES_PER_BLOCK
               + jnp.arange(PAGES_PER_BLOCK, dtype=jnp.int32)).reshape(bd, -1)
    phys = jnp.take_along_axis(page_table, logical, axis=1).reshape(-1)
    o_s = _sample_attn(phys, qkv_s, cache_k[0], cache_v[0]).reshape(rows_s, D_MODEL)
    kv_s = qkv_s.reshape(bd, t_new, 3, N_HEADS, HEAD_DIM)
    k_s = jnp.transpose(kv_s[:, :, 1], (0, 2, 1, 3))
    v_s = jnp.transpose(kv_s[:, :, 2], (0, 2, 1, 3))

    w_o = b_w_o[0].astype(BF16)
    w_r = jnp.pad(moe_w_router[0].T, ((0, 16 - N_EXPERTS), (0, 0))).astype(BF16)
    tail = (w_o, row(ln1_g[1]), row(ln1_b[1]), w_r)
    x3_p, ti_p, tw_p = _attn_out(o_p, xp, *tail, tile=512)
    x3_s, ti_s, tw_s = _attn_out(o_s, xs, *tail, tile=rows_s)

    top_i = jnp.concatenate([ti_p[:TOP_K_EXPERTS], ti_s[:TOP_K_EXPERTS]], axis=1)
    n_slots_used = TOP_K_EXPERTS * (rows_p + rows_s)
    n_tiles = n_slots_used // MOE_TILE + N_EXPERTS
    tile_expert, tile_valid, pos = _route(top_i, n_tiles)
    pos_p, pos_s = pos[:TOP_K_EXPERTS * rows_p], pos[TOP_K_EXPERTS * rows_p:]
    slots = jnp.zeros((n_tiles * MOE_TILE, D_MODEL), F32)
    slots = _dispatch(pos_p, x3_p, slots, tiles_per_step=2)
    slots = _dispatch(pos_s, x3_s, slots, tiles_per_step=1)
    y_slots = _moe(tile_expert, tile_valid, slots, moe_w_gate[0].astype(BF16),
                   moe_w_up[0].astype(BF16), moe_w_down[0].astype(BF16))
    ln2 = (row(ln2_g[1]), row(ln2_b[1]))
    y_prompt = _combine(pos_p, y_slots, x3_p, tw_p[:TOP_K_EXPERTS].T, *ln2).reshape(bsz, seq, D_MODEL)
    y_sample = _combine(pos_s, y_slots, x3_s, tw_s[:TOP_K_EXPERTS].T, *ln2).reshape(bd, t_new, D_MODEL)

    return (y_prompt, y_sample, v_rows.reshape(1, bd, t_new, D_MODEL), k_p[None], v_p[None],
            k_s[None], v_s[None])
```

```python
import functools

import jax
import jax.numpy as jnp
from jax import lax
from jax.experimental import pallas as pl
from jax.experimental.pallas import tpu as pltpu

F32 = jnp.float32
BF16 = jnp.bfloat16

D_MODEL = 1024
DEPTH = 2
CHUNK = 128
A_GROUPS = 8
N_HEADS = 8
HEAD_DIM = 128
MOBA_BLOCK = 256
MOBA_TOPK = 3
PAGE_SIZE = 128
PAGES_PER_BLOCK = MOBA_BLOCK // PAGE_SIZE
N_EXPERTS = 8
TOP_K_EXPERTS = 2
ATTN_SCALE = HEAD_DIM ** -0.5
EXP2_SCALE = ATTN_SCALE * 1.4426950408889634
ALPHA = (2 * DEPTH) ** 0.25
LN_EPS = 1e-5
INV_SQRT2 = 0.7071067811865476
NEG_INF = float("-inf")

LANES = 128
MOE_TILE = 256
ROW_TILE = 128
VMEM_LIMIT = 56 * 1024 * 1024

_NT_DIMS = (((1,), (1,)), ((), ()))


def _ln(x, g, b):
    mu = jnp.mean(x, axis=-1, keepdims=True)
    xc = x - mu
    var = jnp.mean(xc * xc, axis=-1, keepdims=True)
    return xc * lax.rsqrt(var + LN_EPS) * g + b


def _split_bf16(a):
    hi = a.astype(BF16)
    return hi, (a - hi.astype(F32)).astype(BF16)


def _resident(shape):
    nd = len(shape)
    return pl.BlockSpec(shape, lambda *_: (0,) * nd, pipeline_mode=pl.Buffered(1))


def _params(sem, vmem=None):
    return pltpu.CompilerParams(dimension_semantics=sem, vmem_limit_bytes=vmem)


def _gmlp_kernel(x_ref, win_ref, lng_ref, lnb_ref, wmix_ref, bmix_ref, wout_ref, g1_ref, b1_ref,
                 *out_refs, n_chunks, emit_v):
    group = min(n_chunks, 2) * CHUNK
    spans = [(r, r + group) for r in range(0, n_chunks * CHUNK, group)]
    hs = [jnp.dot(x_ref[a:b, :].astype(BF16), win_ref[...], preferred_element_type=F32) for a, b in spans]
    for (a, b), h in zip(spans, hs):
        h = 0.5 * h * (1.0 + lax.erf(h * INV_SQRT2))
        u = h[:, :D_MODEL]
        v = _ln(h[:, D_MODEL:], lng_ref[...], lnb_ref[...])
        if emit_v:
            out_refs[1][a:b, :] = v
        vb = v.astype(BF16)
        rows = []
        for c in range(group // CHUNK):
            cols = []
            for g in range(A_GROUPS):
                vg = vb[c * CHUNK:(c + 1) * CHUNK, g * LANES:(g + 1) * LANES]
                cols.append(jnp.dot(wmix_ref[g], vg, preferred_element_type=F32))
            rows.append(jnp.concatenate(cols, axis=1) + bmix_ref[...])
        z = rows[0] if len(rows) == 1 else jnp.concatenate(rows, axis=0)
        y = jnp.dot((u * z).astype(BF16), wout_ref[...], preferred_element_type=F32)
        out_refs[0][a:b, :] = _ln(ALPHA * x_ref[a:b, :] + y, g1_ref[...], b1_ref[...])


def _gmlp(x, w_in, ln_g, ln_b, wmix, bmix, w_out, g1, b1, *, tile, emit_v):
    rows = x.shape[0]
    n_chunks = tile // CHUNK
    row_spec = pl.BlockSpec((tile, D_MODEL), lambda i: (i, 0))
    out_shape = [jax.ShapeDtypeStruct((rows, D_MODEL), F32)]
    out_specs = [row_spec]
    if emit_v:
        out_shape.append(jax.ShapeDtypeStruct((rows, D_MODEL), F32))
        out_specs.append(row_spec)
    return pl.pallas_call(
        functools.partial(_gmlp_kernel, n_chunks=n_chunks, emit_v=emit_v),
        grid=(rows // tile,),
        in_specs=[row_spec, _resident(w_in.shape), _resident(ln_g.shape), _resident(ln_b.shape),
                  _resident(wmix.shape), _resident(bmix.shape), _resident(w_out.shape),
                  _resident(g1.shape), _resident(b1.shape)],
        out_specs=out_specs,
        out_shape=out_shape,
        compiler_params=_params(("parallel",), VMEM_LIMIT),
        name="gmlp_mixer",
    )(x, w_in, ln_g, ln_b, wmix, bmix, w_out, g1, b1)


def _swiglu_body(xb, wg, wu, wd):
    a = jnp.dot(xb, wg, preferred_element_type=F32)
    up = jnp.dot(xb, wu, preferred_element_type=F32)
    h = (a * jax.nn.sigmoid(a)) * up
    return jnp.dot(h.astype(BF16), wd, preferred_element_type=F32)


def _swiglu_kernel(x_ref, wg_ref, wu_ref, wd_ref, g_ref, b_ref, o_ref):
    tile = x_ref.shape[0]
    group = min(tile, 256)
    spans = [(r, r + group) for r in range(0, tile, group)]
    xbs = [x_ref[a:b, :].astype(BF16) for a, b in spans]
    gates = [(jnp.dot(xb, wg_ref[...], preferred_element_type=F32),
              jnp.dot(xb, wu_ref[...], preferred_element_type=F32)) for xb in xbs]
    for (a, b), (gt, up) in zip(spans, gates):
        h = (gt * jax.nn.sigmoid(gt)) * up
        f = jnp.dot(h.astype(BF16), wd_ref[...], preferred_element_type=F32)
        o_ref[a:b, :] = _ln(ALPHA * x_ref[a:b, :] + f, g_ref[...], b_ref[...])


def _swiglu(x, wg, wu, wd, g, b, *, tile):
    rows = x.shape[0]
    row_spec = pl.BlockSpec((tile, D_MODEL), lambda i: (i, 0))
    return pl.pallas_call(
        _swiglu_kernel,
        grid=(rows // tile,),
        in_specs=[row_spec, _resident(wg.shape), _resident(wu.shape), _resident(wd.shape),
                  _resident(g.shape), _resident(b.shape)],
        out_specs=row_spec,
        out_shape=jax.ShapeDtypeStruct((rows, D_MODEL), F32),
        compiler_params=_params(("parallel",), VMEM_LIMIT),
        name="dense_swiglu",
    )(x, wg, wu, wd, g, b)


QKV_BLOCKS_PER_STEP = 2


def _qkv_prompt_kernel(x_ref, w_ref, qkv_ref, k_ref, v_ref, kmean_ref):
    r = jnp.dot(x_ref[...].astype(BF16), w_ref[...], preferred_element_type=F32)
    qkv_ref[...] = r.astype(BF16)
    for h in range(N_HEADS):
        k_ref[h] = r[:, D_MODEL + h * HEAD_DIM:D_MODEL + (h + 1) * HEAD_DIM]
        v_ref[h] = r[:, 2 * D_MODEL + h * HEAD_DIM:2 * D_MODEL + (h + 1) * HEAD_DIM]
    for n in range(QKV_BLOCKS_PER_STEP):
        kmean_ref[n] = jnp.mean(r[n * MOBA_BLOCK:(n + 1) * MOBA_BLOCK, D_MODEL:2 * D_MODEL], axis=0, keepdims=True)


def _qkv_prompt(x, w_qkv, bsz, seq):
    nb = seq // MOBA_BLOCK
    tile = QKV_BLOCKS_PER_STEP * MOBA_BLOCK
    steps_per_seq = seq // tile
    kv_shape = jax.ShapeDtypeStruct((bsz, N_HEADS, seq, HEAD_DIM), F32)
    kv_spec = pl.BlockSpec((None, N_HEADS, tile, HEAD_DIM), lambda i: (i // steps_per_seq, 0, i % steps_per_seq, 0))
    return pl.pallas_call(
        _qkv_prompt_kernel,
        grid=(bsz * steps_per_seq,),
        in_specs=[pl.BlockSpec((tile, D_MODEL), lambda i: (i, 0)), _resident(w_qkv.shape)],
        out_specs=[pl.BlockSpec((tile, 3 * D_MODEL), lambda i: (i, 0)), kv_spec, kv_spec,
                   pl.BlockSpec((QKV_BLOCKS_PER_STEP, 1, D_MODEL), lambda i: (i, 0, 0))],
        out_shape=[jax.ShapeDtypeStruct((bsz * seq, 3 * D_MODEL), BF16), kv_shape, kv_shape,
                   jax.ShapeDtypeStruct((bsz * nb, 1, D_MODEL), F32)],
        compiler_params=_params(("parallel",), VMEM_LIMIT),
        name="qkv_prompt",
    )(x, w_qkv)


def _qkv_sample_kernel(x_ref, w_ref, o_ref):
    o_ref[...] = jnp.dot(x_ref[...].astype(BF16), w_ref[...], preferred_element_type=F32)


def _qkv_sample(x, w_qkv):
    rows = x.shape[0]
    return pl.pallas_call(
        _qkv_sample_kernel,
        grid=(1,),
        in_specs=[_resident(x.shape), _resident(w_qkv.shape)],
        out_specs=pl.BlockSpec((rows, 3 * D_MODEL), lambda i: (0, 0)),
        out_shape=jax.ShapeDtypeStruct((rows, 3 * D_MODEL), F32),
        compiler_params=_params(("arbitrary",), VMEM_LIMIT),
        name="qkv_sample",
    )(x, w_qkv)


PAIRS_PER_ROUND = 4
STREAM_SLOTS = 4


def _first_argmax(g, idx, size, axis):
    m = jnp.max(g, axis=axis, keepdims=True)
    return jnp.min(jnp.where(g == m, idx, size), axis=axis, keepdims=True)


def _moba_prompt_kernel(it_ref, jt_ref, pt_ref, q_ref, k_ref, v_ref, km_ref, kpool, o_ref, ks_ref,
                        vt_sc, sel_sc, m_sc, l_sc, acc_sc, sa_sc, sb_sc, pbuf, psem,
                        *, n_blocks, n_own_rounds, n_rounds, n_steps):
    nb = n_blocks
    km_hi, km_lo = _split_bf16(km_ref[...])

    step = pl.program_id(0) * N_HEADS + pl.program_id(1)
    n_past_trips = (n_rounds - n_own_rounds) // 2
    chunks_per_step = 1 + n_past_trips
    n_slots, pages_per_chunk = pbuf.shape[:2]
    blocks_per_chunk = pages_per_chunk // PAGES_PER_BLOCK
    last_chunk = n_steps * chunks_per_step - 1

    def chunk_copies(chunk, slot):
        seq = chunk // chunks_per_step
        first = (chunk - seq * chunks_per_step) * pages_per_chunk
        return [pltpu.make_async_copy(kpool.at[pt_ref[seq, first + m]], pbuf.at[slot, m], psem.at[slot])
                for m in range(pages_per_chunk)]

    def stream_chunk(c):
        chunk = step * chunks_per_step + c
        slot = lax.rem(chunk, n_slots)
        for cp in chunk_copies(chunk, slot):
            cp.wait()
        ahead = chunk + n_slots - 1
        for cp in chunk_copies(jnp.minimum(ahead, last_chunk), lax.rem(ahead, n_slots)):
            cp.start()
        for n in range(blocks_per_chunk):
            tot = jnp.sum(pbuf[slot, PAGES_PER_BLOCK * n], axis=1)
            for m in range(1, PAGES_PER_BLOCK):
                tot = tot + jnp.sum(pbuf[slot, PAGES_PER_BLOCK * n + m], axis=1)
            ks_ref[:, pl.ds(c * blocks_per_chunk + n, 1), :] = tot[:, None, :]

    @pl.when(step == 0)
    def _():
        for g in range(n_slots - 1):
            for cp in chunk_copies(g, g):
                cp.start()

    ks_ref[...] = jnp.zeros_like(ks_ref)

    def rows_of(block):
        return pl.ds(pl.multiple_of(block * MOBA_BLOCK, MOBA_BLOCK), MOBA_BLOCK)

    def transpose_v(j, carry):
        vt_sc[j] = v_ref[rows_of(j), :].astype(F32).T.astype(BF16)
        return carry
    lax.fori_loop(0, nb, transpose_v, 0, unroll=4)

    blk_id = lax.broadcasted_iota(jnp.int32, (nb, MOBA_BLOCK), 0)
    key_id = lax.broadcasted_iota(jnp.int32, (MOBA_BLOCK, MOBA_BLOCK), 0)
    qry_id = lax.broadcasted_iota(jnp.int32, (MOBA_BLOCK, MOBA_BLOCK), 1)

    def select_blocks(r, carry):
        tiles = [r * PAIRS_PER_ROUND + u for u in range(PAIRS_PER_ROUND)]
        gates = []
        for i in tiles:
            q = q_ref[rows_of(i), :]
            gates.append(lax.dot_general(km_hi, q, _NT_DIMS, preferred_element_type=F32)
                         + lax.dot_general(km_lo, q, _NT_DIMS, preferred_element_type=F32))
        for i, gate in zip(tiles, gates):
            past = blk_id < i
            g = jnp.where(past, gate, NEG_INF)
            sel = jnp.zeros((nb, MOBA_BLOCK), F32)
            for _ in range(MOBA_TOPK):
                pick = blk_id == _first_argmax(g, blk_id, nb, 0)
                sel = jnp.where(pick, 1.0, sel)
                g = jnp.where(pick, NEG_INF, g)
            sel = jnp.where(past, sel, 0.0)
            for j in range(nb):
                sel_sc[i, j] = sel[j:j + 1, :]
        return carry
    lax.fori_loop(0, n_own_rounds, select_blocks, 0)

    def pair_ids(c):
        return [(it_ref[c * PAIRS_PER_ROUND + u], jt_ref[c * PAIRS_PER_ROUND + u])
                for u in range(PAIRS_PER_ROUND)]

    def score_round(c, dst):
        for u, (i, j) in enumerate(pair_ids(c)):
            dst[u] = lax.dot_general(k_ref[rows_of(j), :], q_ref[rows_of(i), :], _NT_DIMS,
                                     preferred_element_type=F32)

    def softmax_own(c, src):
        ids = pair_ids(c)
        soft = []
        for u in range(PAIRS_PER_ROUND):
            s = jnp.where(key_id <= qry_id, src[u], NEG_INF)
            m = jnp.max(s, axis=0, keepdims=True)
            p = jnp.exp2((s - m) * EXP2_SCALE)
            soft.append((m, jnp.sum(p, axis=0, keepdims=True), p.astype(BF16)))
        pv = [jnp.dot(vt_sc[j], p, preferred_element_type=F32) for (_, j), (_, _, p) in zip(ids, soft)]
        for (i, _), (m, l, _), o in zip(ids, soft, pv):
            m_sc[i] = m
            l_sc[i] = l
            acc_sc[i] = o

    def softmax_past(c, src):
        ids = pair_ids(c)
        old = [(m_sc[i], l_sc[i], acc_sc[i]) for i, _ in ids]
        soft = []
        for u, ((i, j), (m_old, l_old, _)) in enumerate(zip(ids, old)):
            s = jnp.where(sel_sc[i, j] > 0.0, src[u], NEG_INF)
            m_new = jnp.maximum(m_old, jnp.max(s, axis=0, keepdims=True))
            a = jnp.exp2((m_old - m_new) * EXP2_SCALE)
            p = jnp.exp2((s - m_new) * EXP2_SCALE)
            soft.append((m_new, a, a * l_old + jnp.sum(p, axis=0, keepdims=True), p.astype(BF16)))
        pv = [jnp.dot(vt_sc[j], p, preferred_element_type=F32) for (_, j), (_, _, _, p) in zip(ids, soft)]
        for (i, _), (m_new, a, l_new, _), (_, _, acc_old), o in zip(ids, soft, old, pv):
            m_sc[i] = m_new
            l_sc[i] = l_new
            acc_sc[i] = a * acc_old + o

    def two_rounds(softmax_round, stream):
        def body(h, carry):
            c = 2 * h
            score_round(c + 1, sb_sc)
            softmax_round(c, sa_sc)
            score_round(jnp.minimum(c + 2, n_rounds - 1), sa_sc)
            softmax_round(c + 1, sb_sc)
            if stream:
                stream_chunk(h - n_own_rounds // 2 + 1)
            return carry
        return body

    score_round(0, sa_sc)
    lax.fori_loop(0, n_own_rounds // 2, two_rounds(softmax_own, False), 0)
    stream_chunk(0)
    lax.fori_loop(n_own_rounds // 2, n_rounds // 2, two_rounds(softmax_past, True), 0)

    @pl.when(step == n_steps - 1)
    def _():
        for g in range(1, n_slots):
            for cp in chunk_copies(last_chunk, (last_chunk + g) % n_slots):
                cp.wait()

    def finish(i, carry):
        o_ref[rows_of(i), :] = (acc_sc[i] / l_sc[i]).T.astype(o_ref.dtype)
        return carry
    lax.fori_loop(0, nb, finish, 0, unroll=4)


def _pair_rounds(nb):
    assert nb % (2 * PAIRS_PER_ROUND) == 0
    rounds = [[(i, i) for i in range(r, r + PAIRS_PER_ROUND)] for r in range(0, nb, PAIRS_PER_ROUND)]
    n_own = len(rounds)
    remaining = {i: list(range(i)) for i in range(1, nb)}
    while any(remaining.values()):
        tiles = sorted((i for i in remaining if remaining[i]), key=lambda i: -len(remaining[i]))
        rounds.append([(i, remaining[i].pop()) for i in tiles[:PAIRS_PER_ROUND]])
    assert all(len(r) == PAIRS_PER_ROUND for r in rounds) and len(rounds) % 2 == 0
    return rounds, n_own


def _moba_prompt(qkv, kmean, page_table, k_pool, bsz, seq):
    nb = seq // MOBA_BLOCK
    rounds, n_own = _pair_rounds(nb)
    i_tab = jnp.array([i for r in rounds for i, _ in r], jnp.int32)
    j_tab = jnp.array([j for r in rounds for _, j in r], jnp.int32)
    bd, n_pages = page_table.shape
    n_steps = bsz * N_HEADS
    chunks_per_step = 1 + (len(rounds) - n_own) // 2
    assert bd == n_steps and n_pages % (chunks_per_step * PAGES_PER_BLOCK) == 0
    assert n_steps * chunks_per_step >= STREAM_SLOTS
    pages_per_chunk = n_pages // chunks_per_step
    assert n_pages // PAGES_PER_BLOCK <= LANES

    def head_spec(off):
        return pl.BlockSpec((seq, HEAD_DIM), lambda b, h, *_: (b, off + h))

    return pl.pallas_call(
        functools.partial(_moba_prompt_kernel, n_blocks=nb, n_own_rounds=n_own, n_rounds=len(rounds),
                          n_steps=n_steps),
        grid_spec=pltpu.PrefetchScalarGridSpec(
            num_scalar_prefetch=3,
            grid=(bsz, N_HEADS),
            in_specs=[head_spec(0), head_spec(N_HEADS), head_spec(2 * N_HEADS),
                      pl.BlockSpec((None, nb, HEAD_DIM), lambda b, h, *_: (b, 0, h)),
                      pl.BlockSpec(memory_space=pl.ANY)],
            out_specs=[head_spec(0),
                       pl.BlockSpec((None, N_HEADS, LANES, HEAD_DIM),
                                    lambda b, h, *_: (b * N_HEADS + h, 0, 0, 0))],
            scratch_shapes=[pltpu.VMEM((nb, HEAD_DIM, MOBA_BLOCK), BF16),
                            pltpu.VMEM((nb, nb, 1, MOBA_BLOCK), F32),
                            pltpu.VMEM((nb, 1, MOBA_BLOCK), F32),
                            pltpu.VMEM((nb, 1, MOBA_BLOCK), F32),
                            pltpu.VMEM((nb, HEAD_DIM, MOBA_BLOCK), F32),
                            pltpu.VMEM((PAIRS_PER_ROUND, MOBA_BLOCK, MOBA_BLOCK), F32),
                            pltpu.VMEM((PAIRS_PER_ROUND, MOBA_BLOCK, MOBA_BLOCK), F32),
                            pltpu.VMEM((STREAM_SLOTS, pages_per_chunk, N_HEADS, PAGE_SIZE, HEAD_DIM), F32),
                            pltpu.SemaphoreType.DMA((STREAM_SLOTS,))],
        ),
        out_shape=[jax.ShapeDtypeStruct((bsz * seq, D_MODEL), BF16),
                   jax.ShapeDtypeStruct((bd, N_HEADS, LANES, HEAD_DIM), F32)],
        compiler_params=_params(("arbitrary", "arbitrary"), VMEM_LIMIT),
        name="moba_prompt",
    )(i_tab, j_tab, page_table, qkv, qkv, qkv, kmean, k_pool)


def _sample_select_kernel(q_ref, ks_ref, sel_ref, *, n_full):
    t_new = q_ref.shape[0]
    inv = 1.0 / MOBA_BLOCK
    gates = []
    for h in range(N_HEADS):
        qh = jnp.concatenate([q_ref[:, h * HEAD_DIM:(h + 1) * HEAD_DIM],
                              jnp.zeros((8 - t_new, HEAD_DIM), F32)], axis=0)
        q_hi, q_lo = _split_bf16(qh)
        k_hi, k_lo = _split_bf16(ks_ref[h] * inv)
        gates.append(lax.dot_general(q_hi, k_hi, _NT_DIMS, preferred_element_type=F32)
                     + lax.dot_general(q_hi, k_lo, _NT_DIMS, preferred_element_type=F32)
                     + lax.dot_general(q_lo, k_hi, _NT_DIMS, preferred_element_type=F32))
    g = jnp.concatenate(gates, axis=0)
    col = lax.broadcasted_iota(jnp.int32, g.shape, 1)
    g = jnp.where(col < n_full, g, NEG_INF)
    out = jnp.zeros(g.shape, jnp.int32)
    for r in range(MOBA_TOPK):
        idx = _first_argmax(g, col, LANES, 1)
        out = jnp.where(col == r, idx, out)
        g = jnp.where(col == idx, NEG_INF, g)
    sel_ref[...] = out.reshape(sel_ref.shape)


def _sample_select(qkv_s, ksum, n_full):
    bd, t_new, _ = qkv_s.shape
    return pl.pallas_call(
        functools.partial(_sample_select_kernel, n_full=n_full),
        grid=(bd,),
        in_specs=[pl.BlockSpec((None, t_new, D_MODEL), lambda b: (b, 0, 0)),
                  pl.BlockSpec((None, N_HEADS, LANES, HEAD_DIM), lambda b: (b, 0, 0, 0))],
        out_specs=pl.BlockSpec((None, N_HEADS, 8, LANES), lambda b: (b, 0, 0, 0)),
        out_shape=jax.ShapeDtypeStruct((bd, N_HEADS, 8, LANES), jnp.int32),
        compiler_params=_params(("parallel",)),
        name="sample_select",
    )(qkv_s, ksum)


N_GATHER = MOBA_TOPK * PAGES_PER_BLOCK


def _sample_attn_kernel(phys_ref, q_ref, kn_ref, vn_ref, kpool, vpool, o_ref, kbuf, vbuf, sem,
                        *, t_new, n_steps):
    n_in = t_new * N_GATHER
    n_slots = kbuf.shape[0]
    ahead = n_slots - 1
    step = pl.program_id(0) * N_HEADS + pl.program_id(1)
    slot = lax.rem(step, n_slots)

    def page_copies(st, sl):
        head = lax.rem(st, N_HEADS)
        out = []
        for m in range(n_in):
            page = phys_ref[st * n_in + m]
            out.append(pltpu.make_async_copy(kpool.at[page, head], kbuf.at[sl, m], sem.at[sl]))
            out.append(pltpu.make_async_copy(vpool.at[page, head], vbuf.at[sl, m], sem.at[sl]))
        return out

    @pl.when(step == 0)
    def _():
        for st in range(min(ahead, n_steps)):
            for c in page_copies(st, st):
                c.start()

    @pl.when(step + ahead < n_steps)
    def _():
        for c in page_copies(step + ahead, lax.rem(step + ahead, n_slots)):
            c.start()

    for c in page_copies(step, slot):
        c.wait()

    row = lax.broadcasted_iota(jnp.int32, (t_new, 1), 0)
    for t in range(t_new):
        q = q_ref[t:t + 1, :]
        pages = range(t * N_GATHER, (t + 1) * N_GATHER)
        s_own = jnp.sum(kn_ref[...] * q, axis=1, keepdims=True) * ATTN_SCALE
        s_own = jnp.where(row <= t, s_own, NEG_INF)
        s_past = [jnp.sum(kbuf[slot, m] * q, axis=1, keepdims=True) * ATTN_SCALE for m in pages]
        mx = jnp.max(s_own, axis=0, keepdims=True)
        for sp in s_past:
            mx = jnp.maximum(mx, jnp.max(sp, axis=0, keepdims=True))
        p_own = jnp.exp(s_own - mx)
        l = jnp.sum(p_own, axis=0, keepdims=True)
        o = jnp.sum(p_own * vn_ref[...], axis=0, keepdims=True)
        for sp, m in zip(s_past, pages):
            pp = jnp.exp(sp - mx)
            l = l + jnp.sum(pp, axis=0, keepdims=True)
            o = o + jnp.sum(pp * vbuf[slot, m], axis=0, keepdims=True)
        o_ref[t:t + 1, :] = o / l


def _sample_attn(phys, qkv_s, k_pool, v_pool):
    bd, t_new, _ = qkv_s.shape
    n_in = t_new * N_GATHER

    def head_spec(off):
        return pl.BlockSpec((None, t_new, HEAD_DIM), lambda b, h, ph: (b, 0, off + h))

    n_slots = 3
    page_buf = pltpu.VMEM((n_slots, n_in, PAGE_SIZE, HEAD_DIM), F32)
    return pl.pallas_call(
        functools.partial(_sample_attn_kernel, t_new=t_new, n_steps=bd * N_HEADS),
        grid_spec=pltpu.PrefetchScalarGridSpec(
            num_scalar_prefetch=1,
            grid=(bd, N_HEADS),
            in_specs=[head_spec(0), head_spec(N_HEADS), head_spec(2 * N_HEADS),
                      pl.BlockSpec(memory_space=pl.ANY), pl.BlockSpec(memory_space=pl.ANY)],
            out_specs=head_spec(0),
            scratch_shapes=[page_buf, page_buf, pltpu.SemaphoreType.DMA((n_slots,))],
        ),
        out_shape=jax.ShapeDtypeStruct((bd, t_new, D_MODEL), F32),
        compiler_params=_params(("arbitrary", "arbitrary"), VMEM_LIMIT),
        name="sample_attn",
    )(phys, qkv_s, qkv_s, qkv_s, k_pool, v_pool)


def _attn_out_kernel(o_ref, x_ref, wo_ref, g_ref, b_ref, wr_ref, x3_ref, ti_ref, tw_ref):
    tile = o_ref.shape[0]
    group = min(tile, 256)
    spans = [(r, r + group) for r in range(0, tile, group)]
    ms = [jnp.dot(o_ref[a:b, :].astype(BF16), wo_ref[...], preferred_element_type=F32) for a, b in spans]
    for (a, b), m in zip(spans, ms):
        x3 = _ln(ALPHA * x_ref[a:b, :] + m, g_ref[...], b_ref[...])
        x3_ref[a:b, :] = x3
        logits = lax.dot_general(wr_ref[...], x3.astype(BF16), _NT_DIMS, preferred_element_type=F32)
        n_rows = logits.shape[0]
        row = lax.broadcasted_iota(jnp.int32, logits.shape, 0)
        lg = jnp.where(row < N_EXPERTS, logits, NEG_INF)
        v1 = jnp.max(lg, axis=0, keepdims=True)
        i1 = jnp.min(jnp.where(lg == v1, row, n_rows), axis=0, keepdims=True)
        lg2 = jnp.where(row == i1, NEG_INF, lg)
        v2 = jnp.max(lg2, axis=0, keepdims=True)
        i2 = jnp.min(jnp.where(lg2 == v2, row, n_rows), axis=0, keepdims=True)
        e = jnp.exp(v2 - v1)
        w1 = 1.0 / (1.0 + e)
        w2 = e / (1.0 + e)
        out_row = lax.broadcasted_iota(jnp.int32, (ti_ref.shape[0], group), 0)
        ti_ref[:, a:b] = jnp.where(out_row == 0, i1, jnp.where(out_row == 1, i2, 0))
        tw_ref[:, a:b] = jnp.where(out_row == 0, w1, jnp.where(out_row == 1, w2, 0.0))


def _attn_out(o, x, w_o, g, b, w_r, *, tile):
    rows = x.shape[0]
    row_spec = pl.BlockSpec((tile, D_MODEL), lambda i: (i, 0))
    top_spec = pl.BlockSpec((8, tile), lambda i: (0, i))
    return pl.pallas_call(
        _attn_out_kernel,
        grid=(rows // tile,),
        in_specs=[row_spec, row_spec, _resident(w_o.shape), _resident(g.shape), _resident(b.shape),
                  _resident(w_r.shape)],
        out_specs=[row_spec, top_spec, top_spec],
        out_shape=[jax.ShapeDtypeStruct((rows, D_MODEL), F32),
                   jax.ShapeDtypeStruct((8, rows), jnp.int32),
                   jax.ShapeDtypeStruct((8, rows), F32)],
        compiler_params=_params(("parallel",), VMEM_LIMIT),
        name="attn_out_router",
    )(o, x, w_o, g, b, w_r)


def _dispatch_kernel(pos_ref, x_ref, xs_in, xs_out, sem):
    del xs_in
    n = ROW_TILE
    tiles = x_ref.shape[0] // n

    for r in range(n):
        for t in range(tiles):
            for k in range(TOP_K_EXPERTS):
                slot_row = pos_ref[(t * TOP_K_EXPERTS + k) * n + r]
                pltpu.make_async_copy(x_ref.at[pl.ds(t * n + r, 1)], xs_out.at[pl.ds(slot_row, 1)],
                                      sem).start(priority=k % 2)
    for _ in range(TOP_K_EXPERTS):
        pltpu.make_async_copy(x_ref, xs_out.at[pl.ds(0, tiles * n)], sem).wait()


def _dispatch(pos, x, xs, *, tiles_per_step):
    rows = x.shape[0]
    block_rows = tiles_per_step * ROW_TILE
    return pl.pallas_call(
        _dispatch_kernel,
        grid=(rows // block_rows,),
        in_specs=[pl.BlockSpec((TOP_K_EXPERTS * block_rows,), lambda i: (i,), memory_space=pltpu.SMEM),
                  pl.BlockSpec((block_rows, D_MODEL), lambda i: (i, 0)),
                  pl.BlockSpec(memory_space=pl.ANY)],
        out_specs=pl.BlockSpec(memory_space=pl.ANY),
        out_shape=jax.ShapeDtypeStruct(xs.shape, xs.dtype),
        scratch_shapes=[pltpu.SemaphoreType.DMA(())],
        input_output_aliases={2: 0},
        compiler_params=_params(("arbitrary",)),
        name="moe_dispatch",
    )(pos, x, xs)


def _moe_kernel(te_ref, tv_ref, xs_ref, wg_ref, wu_ref, wd_ref, y_ref):
    i = pl.program_id(0)

    @pl.when(tv_ref[i] == 1)
    def _():
        y_ref[...] = _swiglu_body(xs_ref[...].astype(BF16), wg_ref[...], wu_ref[...], wd_ref[...])

    @pl.when(tv_ref[i] == 0)
    def _():
        y_ref[...] = jnp.zeros_like(y_ref)


def _moe(tile_expert, tile_valid, xs, wg, wu, wd):
    n_tiles = tile_expert.shape[0]
    d_ff = wg.shape[2]
    tile_spec = pl.BlockSpec((MOE_TILE, D_MODEL), lambda i, te, tv: (i, 0))
    return pl.pallas_call(
        _moe_kernel,
        grid_spec=pltpu.PrefetchScalarGridSpec(
            num_scalar_prefetch=2,
            grid=(n_tiles,),
            in_specs=[tile_spec,
                      pl.BlockSpec((None, D_MODEL, d_ff), lambda i, te, tv: (te[i], 0, 0)),
                      pl.BlockSpec((None, D_MODEL, d_ff), lambda i, te, tv: (te[i], 0, 0)),
                      pl.BlockSpec((None, d_ff, D_MODEL), lambda i, te, tv: (te[i], 0, 0))],
            out_specs=tile_spec,
        ),
        out_shape=jax.ShapeDtypeStruct((n_tiles * MOE_TILE, D_MODEL), F32),
        compiler_params=_params(("arbitrary",), VMEM_LIMIT),
        name="moe_experts",
    )(tile_expert, tile_valid, xs, wg, wu, wd)


COMBINE_SLOTS = 3


def _combine_kernel(pos_ref, pos1_ref, pos2_ref, y_hbm, x_ref, tw_ref, g_ref, b_ref, o_ref, ybuf, sem, *, n_steps):
    i = pl.program_id(0)
    slot = lax.rem(i, COMBINE_SLOTS)
    n = x_ref.shape[0]
    n_rows = TOP_K_EXPERTS * n

    def gather(idx_ref, sl):
        for r in range(n_rows):
            pltpu.make_async_copy(y_hbm.at[pl.ds(idx_ref[r], 1)], ybuf.at[sl, pl.ds(r, 1)],
                                  sem.at[sl]).start(priority=r % 2)

    def wait_rows(sl):
        pltpu.make_async_copy(y_hbm.at[pl.ds(0, n_rows)], ybuf.at[sl], sem.at[sl]).wait()

    @pl.when(i == 0)
    def _():
        gather(pos_ref, 0)
        gather(pos1_ref, 1)

    wait_rows(slot)
    gather(pos2_ref, lax.rem(i + 2, COMBINE_SLOTS))
    tw = tw_ref[...]
    moe = tw[:, 0:1] * ybuf[slot, :n, :] + tw[:, 1:2] * ybuf[slot, n:, :]
    o_ref[...] = _ln(ALPHA * x_ref[...] + moe, g_ref[...], b_ref[...])

    @pl.when(i == n_steps - 1)
    def _():
        wait_rows(lax.rem(i + 1, COMBINE_SLOTS))
        wait_rows(lax.rem(i + 2, COMBINE_SLOTS))


def _combine(pos, y_sorted, x3, tw, g, b):
    rows = x3.shape[0]
    n_steps = rows // ROW_TILE
    row_spec = pl.BlockSpec((ROW_TILE, D_MODEL), lambda i: (i, 0))

    def idx_spec(ahead):
        return pl.BlockSpec((TOP_K_EXPERTS * ROW_TILE,), lambda i: (jnp.minimum(i + ahead, n_steps - 1),),
                            memory_space=pltpu.SMEM)

    return pl.pallas_call(
        functools.partial(_combine_kernel, n_steps=n_steps),
        grid=(n_steps,),
        in_specs=[idx_spec(0), idx_spec(1), idx_spec(2),
                  pl.BlockSpec(memory_space=pl.ANY), row_spec,
                  pl.BlockSpec((ROW_TILE, TOP_K_EXPERTS), lambda i: (i, 0)),
                  _resident(g.shape), _resident(b.shape)],
        out_specs=row_spec,
        out_shape=jax.ShapeDtypeStruct((rows, D_MODEL), F32),
        scratch_shapes=[pltpu.VMEM((COMBINE_SLOTS, TOP_K_EXPERTS * ROW_TILE, D_MODEL), F32),
                        pltpu.SemaphoreType.DMA((COMBINE_SLOTS,))],
        compiler_params=_params(("arbitrary",), VMEM_LIMIT),
        name="moe_combine",
    )(pos, pos, pos, y_sorted, x3, tw, g, b)


def _route(top_i, n_tiles):
    rows = top_i.shape[1]
    e_flat = top_i.reshape(-1)
    experts = jnp.arange(N_EXPERTS, dtype=jnp.int32)
    onehot = (e_flat[None, :] == experts[:, None]).astype(jnp.int32)
    csum = jnp.cumsum(onehot, axis=1)
    rank = jnp.sum(onehot * csum, axis=0) - 1
    tiles_e = (csum[:, -1] + MOE_TILE - 1) // MOE_TILE
    tile_end = jnp.cumsum(tiles_e)
    tile_start = tile_end - tiles_e
    pos = jnp.sum(onehot * tile_start[:, None], axis=0) * MOE_TILE + rank
    tile_id = jnp.arange(n_tiles, dtype=jnp.int32)
    tile_expert = jnp.sum((tile_id[:, None] >= tile_end[None, :]).astype(jnp.int32), axis=1)
    tile_expert = jnp.minimum(tile_expert, N_EXPERTS - 1)
    tile_valid = (tile_id < tile_end[-1]).astype(jnp.int32)
    pos_tiles = pos.reshape(TOP_K_EXPERTS, rows // ROW_TILE, ROW_TILE).transpose(1, 0, 2).reshape(-1)
    return tile_expert, tile_valid, pos_tiles.astype(jnp.int32)


def kernel(x_prompt, x_sample, cache_k, cache_v, page_table, a_w_in, a_ln_g, a_ln_b, a_w_s, a_b_s, a_w_out,
           b_w_qkv, b_w_o, ffn_w_gate, ffn_w_up, ffn_w_down, moe_w_router, moe_w_gate, moe_w_up, moe_w_down,
           ln1_g, ln1_b, ln2_g, ln2_b):
    bsz, seq, _ = x_prompt.shape
    bd, t_new, _ = x_sample.shape
    rows_p, rows_s = bsz * seq, bd * t_new
    assert rows_s == CHUNK and CHUNK % t_new == 0 and seq % MOBA_BLOCK == 0
    n_pages = page_table.shape[1]
    past = n_pages * PAGE_SIZE
    assert past % MOBA_BLOCK == 0
    n_full = past // MOBA_BLOCK
    assert MOBA_TOPK <= n_full <= LANES

    xp = x_prompt.reshape(rows_p, D_MODEL)
    xs = x_sample.reshape(rows_s, D_MODEL)
    row = lambda a: a.reshape(1, -1)

    w_in = a_w_in[0].astype(BF16)
    w_out = a_w_out[0].astype(BF16)
    tril = jnp.tril(jnp.ones((CHUNK, CHUNK), bool))
    wmix_p = jnp.where(tril, a_w_s[0], 0.0).astype(BF16)
    bmix_p = jnp.repeat(a_b_s[0].T, LANES, axis=1)
    w_small = jnp.where(tril[:t_new, :t_new], a_w_s[0][:, :t_new, :t_new], 0.0)
    wmix_s = jnp.einsum("ab,gij->gaibj", jnp.eye(bd, dtype=F32), w_small).reshape(
        A_GROUPS, rows_s, rows_s).astype(BF16)
    bmix_s = jnp.repeat(jnp.tile(a_b_s[0][:, :t_new], (1, bd)).T, LANES, axis=1)
    gm_args = (w_in, row(a_ln_g[0]), row(a_ln_b[0]))
    gm_tail = (w_out, row(ln1_g[0]), row(ln1_b[0]))
    (xp,) = _gmlp(xp, *gm_args, wmix_p, bmix_p, *gm_tail, tile=1024, emit_v=False)
    xs, v_rows = _gmlp(xs, *gm_args, wmix_s, bmix_s, *gm_tail, tile=rows_s, emit_v=True)

    ffn = (ffn_w_gate[0].astype(BF16), ffn_w_up[0].astype(BF16), ffn_w_down[0].astype(BF16),
           row(ln2_g[0]), row(ln2_b[0]))
    xp = _swiglu(xp, *ffn, tile=512)
    xs = _swiglu(xs, *ffn, tile=rows_s)

    w_qkv = b_w_qkv[0].astype(BF16)
    qkv_p, k_p, v_p, kmean = _qkv_prompt(xp, w_qkv, bsz, seq)
    o_p, ksum = _moba_prompt(qkv_p, kmean.reshape(bsz, seq // MOBA_BLOCK, D_MODEL), page_table, cache_k[0],
                             bsz, seq)

    qkv_s = _qkv_sample(xs, w_qkv).reshape(bd, t_new, 3 * D_MODEL)
    sel = _sample_select(qkv_s, ksum, n_full)[:, :, :t_new, :MOBA_TOPK]
    logical = (sel[..., None] * PAGES_PER_BLOCK
               + jnp.arange(PAGES_PER_BLOCK, dtype=jnp.int32)).reshape(bd, -1)
    phys = jnp.take_along_axis(page_table, logical, axis=1).reshape(-1)
    o_s = _sample_attn(phys, qkv_s, cache_k[0], cache_v[0]).reshape(rows_s, D_MODEL)
    kv_s = qkv_s.reshape(bd, t_new, 3, N_HEADS, HEAD_DIM)
    k_s = jnp.transpose(kv_s[:, :, 1], (0, 2, 1, 3))
    v_s = jnp.transpose(kv_s[:, :, 2], (0, 2, 1, 3))

    w_o = b_w_o[0].astype(BF16)
    w_r = jnp.pad(moe_w_router[0].T, ((0, 16 - N_EXPERTS), (0, 0))).astype(BF16)
    tail = (w_o, row(ln1_g[1]), row(ln1_b[1]), w_r)
    x3_p, ti_p, tw_p = _attn_out(o_p, xp, *tail, tile=1024)
    x3_s, ti_s, tw_s = _attn_out(o_s, xs, *tail, tile=rows_s)

    top_i = jnp.concatenate([ti_p[:TOP_K_EXPERTS], ti_s[:TOP_K_EXPERTS]], axis=1)
    n_slots_used = TOP_K_EXPERTS * (rows_p + rows_s)
    n_tiles = n_slots_used // MOE_TILE + N_EXPERTS
    tile_expert, tile_valid, pos = _route(top_i, n_tiles)
    pos_p, pos_s = pos[:TOP_K_EXPERTS * rows_p], pos[TOP_K_EXPERTS * rows_p:]
    slots = jnp.zeros((n_tiles * MOE_TILE, D_MODEL), F32)
    slots = _dispatch(pos_p, x3_p, slots, tiles_per_step=4)
    slots = _dispatch(pos_s, x3_s, slots, tiles_per_step=1)
    y_slots = _moe(tile_expert, tile_valid, slots, moe_w_gate[0].astype(BF16),
                   moe_w_up[0].astype(BF16), moe_w_down[0].astype(BF16))
    ln2 = (row(ln2_g[1]), row(ln2_b[1]))
    y_prompt = _combine(pos_p, y_slots, x3_p, tw_p[:TOP_K_EXPERTS].T, *ln2).reshape(bsz, seq, D_MODEL)
    y_sample = _combine(pos_s, y_slots, x3_s, tw_s[:TOP_K_EXPERTS].T, *ln2).reshape(bd, t_new, D_MODEL)

    return (y_prompt, y_sample, v_rows.reshape(1, bd, t_new, D_MODEL), k_p[None], v_p[None],
            k_s[None], v_s[None])
```
